```python
import jax, jax.numpy as jnp
from jax import lax
import numpy as np

D_MODEL = 1024
BATCH = 8
SEQ = 4096
DEPTH = 4

GRID_W = 64
CTX_LEN = 256
N_MIXERS = 2
N_CONV_LAYERS = (DEPTH + N_MIXERS - 1) // N_MIXERS
N_ATTN_LAYERS = DEPTH // N_MIXERS
HEAD_DIM = 128
N_HEADS = D_MODEL // HEAD_DIM
N_KV_HEADS = N_HEADS // 4
GQA_GROUP = N_HEADS // N_KV_HEADS
QKV_DIM = (N_HEADS + 2 * N_KV_HEADS) * HEAD_DIM
ROPE_THETA = 10000.0
ROPE_FREQS = HEAD_DIM // 4
Q_BLOCK = 128
D_FF = 2816
CONV_WIDTH = 3
N_SUBLAYERS = 3
N_MOD = 3
NORM_EPS = 1e-6
MACARON_WEIGHT = 0.5

kernel_name = "hybrid_conv_gqa_macaron_dit"


def rms_norm(x, g):
    xf = x.astype(jnp.float32)
    y = xf * lax.rsqrt(jnp.mean(xf * xf, axis=-1, keepdims=True) + NORM_EPS)
    return (y * g.astype(jnp.float32)).astype(x.dtype)


def norm_modulate(x, g, shift, scale):
    return rms_norm(x, g) * (1 + scale) + shift


def swiglu(h, w_in, w_out):
    gate, up = jnp.split(h @ w_in, 2, axis=-1)
    return (jax.nn.silu(gate) * up) @ w_out


def short_conv_mixer(h, w_in, w_conv, w_out):
    b_gate, c_gate, v = jnp.split(h @ w_in, 3, axis=-1)
    u = c_gate * v
    n = u.shape[1]
    half = CONV_WIDTH // 2
    u_pad = jnp.pad(u, ((0, 0), (half, half), (0, 0)))
    y = w_conv[0] * u_pad[:, 0:n]
    for k in range(1, CONV_WIDTH):
        y = y + w_conv[k] * u_pad[:, k:k + n]
    return (b_gate * y) @ w_out


def grid_rope_tables(n, dtype):
    rows_count = n // GRID_W
    row = jnp.broadcast_to(jnp.arange(rows_count, dtype=jnp.int32)[:, None], (rows_count, GRID_W)).reshape(-1)
    col = jnp.broadcast_to(jnp.arange(GRID_W, dtype=jnp.int32)[None, :], (rows_count, GRID_W)).reshape(-1)
    inv_freq = ROPE_THETA ** (-jnp.arange(ROPE_FREQS, dtype=jnp.float32) / ROPE_FREQS)
    ang = jnp.stack([row.astype(jnp.float32)[:, None] * inv_freq,
                     col.astype(jnp.float32)[:, None] * inv_freq], axis=1)
    ang = ang[:, None]
    return jnp.cos(ang).astype(dtype), jnp.sin(ang).astype(dtype)


def apply_rope_2d(x, cos, sin):
    b, n, h, d = x.shape
    xr = x.reshape(b, n, h, 2, 2, ROPE_FREQS)
    x1, x2 = xr[..., 0, :], xr[..., 1, :]
    out = jnp.stack([x1 * cos - x2 * sin, x2 * cos + x1 * sin], axis=-2)
    return out.reshape(b, n, h, d)


def project_qkv(h, w_qkv, q_g, k_g):
    b, n, _ = h.shape
    qkv = h @ w_qkv
    q = qkv[..., :N_HEADS * HEAD_DIM].reshape(b, n, N_HEADS, HEAD_DIM)
    k = qkv[..., N_HEADS * HEAD_DIM:(N_HEADS + N_KV_HEADS) * HEAD_DIM].reshape(b, n, N_KV_HEADS, HEAD_DIM)
    v = qkv[..., (N_HEADS + N_KV_HEADS) * HEAD_DIM:].reshape(b, n, N_KV_HEADS, HEAD_DIM)
    return rms_norm(q, q_g), rms_norm(k, k_g), v


def gqa_block(q, k, v):
    s = jnp.einsum('bqkgd,bskd->bkgqs', q, k).astype(jnp.float32) * (HEAD_DIM ** -0.5)
    p = jax.nn.softmax(s, axis=-1).astype(v.dtype)
    return jnp.einsum('bkgqs,bskd->bqkgd', p, v)


def attention_mixer(h_lat, h_ctx, w_qkv, q_g, k_g, w_o, cos, sin, need_ctx_out):
    b, n, _ = h_lat.shape
    q_l, k_l, v_l = project_qkv(h_lat, w_qkv, q_g, k_g)
    q_c, k_c, v_c = project_qkv(h_ctx, w_qkv, q_g, k_g)
    q_l = apply_rope_2d(q_l, cos, sin)
    k_l = apply_rope_2d(k_l, cos, sin)
    k_all = jnp.concatenate([k_c, k_l], axis=1)
    v_all = jnp.concatenate([v_c, v_l], axis=1)
    n_blocks = n // Q_BLOCK
    qb = q_l.reshape(b, n_blocks, Q_BLOCK, N_KV_HEADS, GQA_GROUP, HEAD_DIM).transpose(1, 0, 2, 3, 4, 5)
    o_l = lax.map(lambda q_blk: gqa_block(q_blk, k_all, v_all), qb)
    o_l = o_l.transpose(1, 0, 2, 3, 4, 5).reshape(b, n, N_HEADS * HEAD_DIM)
    y_lat = o_l @ w_o
    y_ctx = None
    if need_ctx_out:
        o_c = gqa_block(q_c.reshape(b, q_c.shape[1], N_KV_HEADS, GQA_GROUP, HEAD_DIM), k_c, v_c)
        y_ctx = o_c.reshape(b, q_c.shape[1], N_HEADS * HEAD_DIM) @ w_o
    return y_lat, y_ctx


def setup_inputs(seed: int = 0) -> dict:
    key = jax.random.key(seed)
    ks = jax.random.split(key, 20)
    d, f = D_MODEL, D_FF
    nrm = jax.random.normal
    return {
        "x": nrm(ks[0], (BATCH, SEQ, d), jnp.float32),
        "c": nrm(ks[1], (BATCH, d), jnp.float32),
        "ctx": nrm(ks[2], (BATCH, CTX_LEN, d), jnp.float32),
        "c_ctx": nrm(ks[3], (d,), jnp.float32),
        "ada_w": nrm(ks[4], (DEPTH, d, N_SUBLAYERS * N_MOD * d), jnp.float32) * (0.5 * d ** -0.5),
        "ada_b": nrm(ks[5], (DEPTH, N_SUBLAYERS * N_MOD * d), jnp.float32) * 0.02,
        "norm_g": 1.0 + 0.02 * nrm(ks[6], (DEPTH, N_SUBLAYERS, d), jnp.float32),
        "final_g": 1.0 + 0.02 * nrm(ks[7], (d,), jnp.float32),
        "ffn_w_in": nrm(ks[8], (DEPTH, 2, d, 2 * f), jnp.float32) * d ** -0.5,
        "ffn_w_out": nrm(ks[9], (DEPTH, 2, f, d), jnp.float32) * f ** -0.5,
        "conv_w_in": nrm(ks[10], (N_CONV_LAYERS, d, 3 * d), jnp.float32) * d ** -0.5,
        "conv_w": nrm(ks[11], (N_CONV_LAYERS, CONV_WIDTH, d), jnp.float32) * CONV_WIDTH ** -0.5,
        "conv_w_out": nrm(ks[12], (N_CONV_LAYERS, d, d), jnp.float32) * d ** -0.5,
        "attn_w_qkv": nrm(ks[13], (N_ATTN_LAYERS, d, QKV_DIM), jnp.float32) * d ** -0.5,
        "attn_q_g": 1.0 + 0.02 * nrm(ks[14], (N_ATTN_LAYERS, HEAD_DIM), jnp.float32),
        "attn_k_g": 1.0 + 0.02 * nrm(ks[15], (N_ATTN_LAYERS, HEAD_DIM), jnp.float32),
        "attn_w_o": nrm(ks[16], (N_ATTN_LAYERS, N_HEADS * HEAD_DIM, d), jnp.float32) * (N_HEADS * HEAD_DIM) ** -0.5,
    }


def reference(x, c, ctx, c_ctx, ada_w, ada_b, norm_g, final_g, ffn_w_in, ffn_w_out,
              conv_w_in, conv_w, conv_w_out, attn_w_qkv, attn_q_g, attn_k_g, attn_w_o):
    b, n, d = x.shape
    cos, sin = grid_rope_tables(n, x.dtype)
    silu_c = jax.nn.silu(c)
    silu_cc = jax.nn.silu(c_ctx)
    x_lat, x_ctx = x, ctx
    for i in range(DEPTH):
        is_attn = (i % N_MIXERS) == 1
        last = i == DEPTH - 1
        ctx_feeds_mixer = (not last) or is_attn
        m_lat = (silu_c @ ada_w[i] + ada_b[i]).reshape(b, N_SUBLAYERS, N_MOD, 1, d)
        m_ctx = (silu_cc @ ada_w[i] + ada_b[i]).reshape(N_SUBLAYERS, N_MOD, 1, d)
        h = norm_modulate(x_lat, norm_g[i, 0], m_lat[:, 0, 0], m_lat[:, 0, 1])
        x_lat = x_lat + MACARON_WEIGHT * m_lat[:, 0, 2] * swiglu(h, ffn_w_in[i, 0], ffn_w_out[i, 0])
        if ctx_feeds_mixer:
            hc = norm_modulate(x_ctx, norm_g[i, 0], m_ctx[0, 0], m_ctx[0, 1])
            x_ctx = x_ctx + MACARON_WEIGHT * m_ctx[0, 2] * swiglu(hc, ffn_w_in[i, 0], ffn_w_out[i, 0])
        h = norm_modulate(x_lat, norm_g[i, 1], m_lat[:, 1, 0], m_lat[:, 1, 1])
        if is_attn:
            j = i // N_MIXERS
            hc = norm_modulate(x_ctx, norm_g[i, 1], m_ctx[1, 0], m_ctx[1, 1])
            y_lat, y_ctx = attention_mixer(h, hc, attn_w_qkv[j], attn_q_g[j], attn_k_g[j], attn_w_o[j],
                                           cos, sin, not last)
        else:
            j = i // N_MIXERS
            y_lat = short_conv_mixer(h, conv_w_in[j], conv_w[j], conv_w_out[j])
            y_ctx = None
            if not last:
                hc = norm_modulate(x_ctx, norm_g[i, 1], m_ctx[1, 0], m_ctx[1, 1])
                y_ctx = short_conv_mixer(hc, conv_w_in[j], conv_w[j], conv_w_out[j])
        x_lat = x_lat + m_lat[:, 1, 2] * y_lat
        if not last:
            x_ctx = x_ctx + m_ctx[1, 2] * y_ctx
        h = norm_modulate(x_lat, norm_g[i, 2], m_lat[:, 2, 0], m_lat[:, 2, 1])
        x_lat = x_lat + MACARON_WEIGHT * m_lat[:, 2, 2] * swiglu(h, ffn_w_in[i, 1], ffn_w_out[i, 1])
        if not last:
            hc = norm_modulate(x_ctx, norm_g[i, 2], m_ctx[2, 0], m_ctx[2, 1])
            x_ctx = x_ctx + MACARON_WEIGHT * m_ctx[2, 2] * swiglu(hc, ffn_w_in[i, 1], ffn_w_out[i, 1])
    return rms_norm(x_lat, final_g)
```

```python
import functools
import math

import jax
import jax.numpy as jnp
from jax import lax
from jax.experimental import pallas as pl
from jax.experimental.pallas import tpu as pltpu

D_MODEL = 1024
DEPTH = 4
GRID_W = 64
N_MIXERS = 2
HEAD_DIM = 128
N_HEADS = D_MODEL // HEAD_DIM
N_KV_HEADS = N_HEADS // 4
GQA_GROUP = N_HEADS // N_KV_HEADS
KV_DIM = N_KV_HEADS * HEAD_DIM
QKV_DIM = (N_HEADS + 2 * N_KV_HEADS) * HEAD_DIM
ROPE_THETA = 10000.0
ROPE_FREQS = HEAD_DIM // 4
D_FF = 2816
N_SUBLAYERS = 3
N_MOD = 3
NORM_EPS = 1e-6
MACARON_WEIGHT = 0.5

VMEM_LIMIT_BYTES = 56 * 1024 * 1024
MOD_ROWS = 16
ADA_COL_BLOCK = 1024
FFN_CHUNKS = ((0, 1024), (1024, 2048), (2048, D_FF))


def _params(n_grid_dims):
    return pltpu.CompilerParams(
        dimension_semantics=("arbitrary",) * n_grid_dims,
        vmem_limit_bytes=VMEM_LIMIT_BYTES)


def _resident(shape):
    nd = len(shape)
    return pl.BlockSpec(shape, lambda *_: (0,) * nd, pipeline_mode=pl.Buffered(1))


def _sigmoid(x):
    return 1.0 / (1.0 + jnp.exp(-x))


def _norm_modulate(x, g, mod_ref):
    ms = jnp.mean(x * x, axis=-1, keepdims=True)
    a = g * (1.0 + mod_ref[1:2, :])
    h = (x * lax.rsqrt(ms + NORM_EPS)) * a + mod_ref[0:1, :]
    return h.astype(jnp.bfloat16)


def _dot(a, b):
    return jnp.dot(a, b, preferred_element_type=jnp.float32)


def _ada_kernel(c_ref, w_ref, b_ref, o_ref):
    c = c_ref[...]
    s = (c * _sigmoid(c)).astype(jnp.bfloat16)
    o_ref[...] = _dot(s, w_ref[...].astype(jnp.bfloat16)) + b_ref[...]


def _ada_mods(cvec, ada_w, ada_b):
    depth, d, n = ada_w.shape
    return pl.pallas_call(
        _ada_kernel,
        grid=(depth, n // ADA_COL_BLOCK),
        in_specs=[
            pl.BlockSpec((MOD_ROWS, d), lambda i, j: (0, 0)),
            pl.BlockSpec((None, d, ADA_COL_BLOCK), lambda i, j: (i, 0, j)),
            pl.BlockSpec((None, 1, ADA_COL_BLOCK), lambda i, j: (i, 0, j)),
        ],
        out_specs=pl.BlockSpec((None, MOD_ROWS, ADA_COL_BLOCK), lambda i, j: (i, 0, j)),
        out_shape=jax.ShapeDtypeStruct((depth, MOD_ROWS, n), jnp.float32),
        compiler_params=_params(2),
        name="ada_mods",
    )(cvec, ada_w, ada_b.reshape(depth, 1, n))


def _mod_spec(per_group):
    if per_group:
        return pl.BlockSpec((None, N_MOD, D_MODEL), lambda b, j: (b, 0, 0))
    return pl.BlockSpec((None, N_MOD, D_MODEL), lambda b, j: (0, 0, 0))


def _row_spec(tm, width=D_MODEL):
    return pl.BlockSpec((None, tm, width), lambda b, j: (b, j, 0))


def _ffn_kernel(x_ref, mod_ref, g_ref, w_in_ref, w_out_ref, *rest, final_norm):
    if final_norm:
        fg_ref, o_ref = rest
    else:
        (o_ref,) = rest
    x = x_ref[...]
    h = _norm_modulate(x, g_ref[...], mod_ref)
    acc = None
    for lo, hi in FFN_CHUNKS:
        gate = _dot(h, w_in_ref[:, lo:hi])
        up = _dot(h, w_in_ref[:, D_FF + lo:D_FF + hi])
        act = (gate * _sigmoid(gate) * up).astype(jnp.bfloat16)
        part = _dot(act, w_out_ref[lo:hi, :])
        acc = part if acc is None else acc + part
    y = x + (MACARON_WEIGHT * mod_ref[2:3, :]) * acc
    if final_norm:
        ms = jnp.mean(y * y, axis=-1, keepdims=True)
        y = (y * lax.rsqrt(ms + NORM_EPS)) * fg_ref[...]
    o_ref[...] = y


def _ffn(x, mod, g, w_in, w_out, *, tm, final_g=None):
    groups, t, d = x.shape
    in_specs = [_row_spec(tm), _mod_spec(mod.shape[0] > 1), _resident((1, d)),
                _resident(w_in.shape), _resident(w_out.shape)]
    args = [x, mod, g.reshape(1, d), w_in, w_out]
    if final_g is not None:
        in_specs.append(_resident((1, d)))
        args.append(final_g.reshape(1, d))
    return pl.pallas_call(
        functools.partial(_ffn_kernel, final_norm=final_g is not None),
        grid=(groups, t // tm),
        in_specs=in_specs,
        out_specs=_row_spec(tm),
        out_shape=jax.ShapeDtypeStruct(x.shape, jnp.float32),
        compiler_params=_params(2),
        name="ffn",
    )(*args)


HALO = 8


def _conv_kernel(x_ref, xp_ref, xn_ref, mod_ref, g_ref, w_in_ref, cw_ref, w_out_ref, o_ref):
    j = pl.program_id(1)
    nj = pl.num_programs(1)
    d = D_MODEL
    g = g_ref[...]
    x = x_ref[...]
    tm = x.shape[0]
    h = _norm_modulate(x, g, mod_ref)
    bcv = _dot(h, w_in_ref[...])
    u = bcv[:, d:2 * d] * bcv[:, 2 * d:]
    hp = _norm_modulate(xp_ref[...], g, mod_ref)
    cvp = _dot(hp, w_in_ref[:, d:])
    up = (cvp[:, :d] * cvp[:, d:])[HALO - 1:HALO, :]
    up = jnp.where(j > 0, up, 0.0)
    hn = _norm_modulate(xn_ref[...], g, mod_ref)
    cvn = _dot(hn, w_in_ref[:, d:])
    un = (cvn[:, :d] * cvn[:, d:])[0:1, :]
    un = jnp.where(j < nj - 1, un, 0.0)
    row = lax.broadcasted_iota(jnp.int32, (tm, 1), 0)
    u_prev = jnp.where(row == 0, up, pltpu.roll(u, 1, 0))
    u_next = jnp.where(row == tm - 1, un, pltpu.roll(u, tm - 1, 0))
    y = cw_ref[0:1, :] * u_prev + cw_ref[1:2, :] * u + cw_ref[2:3, :] * u_next
    z = (bcv[:, :d] * y).astype(jnp.bfloat16)
    o_ref[...] = x + mod_ref[2:3, :] * _dot(z, w_out_ref[...])


def _conv_mixer(x, mod, g, w_in, conv_w, w_out, *, tm):
    groups, t, d = x.shape
    nj = t // tm
    hb = tm // HALO
    last_halo_block = t // HALO - 1
    prev_spec = pl.BlockSpec((None, HALO, d), lambda b, j: (b, jnp.maximum(j * hb - 1, 0), 0))
    next_spec = pl.BlockSpec((None, HALO, d),
                             lambda b, j: (b, jnp.minimum((j + 1) * hb, last_halo_block), 0))
    return pl.pallas_call(
        _conv_kernel,
        grid=(groups, nj),
        in_specs=[_row_spec(tm), prev_spec, next_spec, _mod_spec(mod.shape[0] > 1),
                  _resident((1, d)), _resident(w_in.shape), _resident(conv_w.shape),
                  _resident(w_out.shape)],
        out_specs=_row_spec(tm),
        out_shape=jax.ShapeDtypeStruct(x.shape, jnp.float32),
        compiler_params=_params(2),
        name="conv_mixer",
    )(x, x, x, mod, g.reshape(1, d), w_in, conv_w, w_out)


def _head_norm(xh, g):
    ms = jnp.mean(xh * xh, axis=-1, keepdims=True)
    return (xh * lax.rsqrt(ms + NORM_EPS)) * g


def _rope(xh, cos, sin_lo, sin_hi):
    return (xh * cos + pltpu.roll(xh, HEAD_DIM - ROPE_FREQS, 1) * sin_lo
            + pltpu.roll(xh, ROPE_FREQS, 1) * sin_hi)


def _qkv_kernel(x_ref, mod_ref, g_ref, w_ref, qg_ref, kg_ref, *rest, rope, q_scale):
    if rope:
        cos_ref, slo_ref, shi_ref, q_ref, k_ref, v_ref = rest
        cos, slo, shi = cos_ref[...], slo_ref[...], shi_ref[...]
    else:
        q_ref, k_ref, v_ref = rest
    h = _norm_modulate(x_ref[...], g_ref[...], mod_ref)
    qkv = _dot(h, w_ref[...])
    qg = qg_ref[...]
    kg = kg_ref[...]
    for hd in range(N_HEADS + N_KV_HEADS):
        sl = slice(hd * HEAD_DIM, (hd + 1) * HEAD_DIM)
        is_q = hd < N_HEADS
        xh = _head_norm(qkv[:, sl], qg if is_q else kg)
        if rope:
            xh = _rope(xh, cos, slo, shi)
        if is_q:
            q_ref[:, sl] = (xh * q_scale).astype(q_ref.dtype)
        else:
            ks = slice((hd - N_HEADS) * HEAD_DIM, (hd - N_HEADS + 1) * HEAD_DIM)
            k_ref[:, ks] = xh.astype(k_ref.dtype)
    v_ref[...] = qkv[:, D_MODEL + KV_DIM:].astype(v_ref.dtype)


def _qkv_proj(x, mod, g, w_qkv, q_g, k_g, rope_tabs, *, tm, q_scale):
    groups, t, d = x.shape
    rope = rope_tabs is not None
    in_specs = [_row_spec(tm), _mod_spec(mod.shape[0] > 1), _resident((1, d)),
                _resident(w_qkv.shape), _resident((1, HEAD_DIM)), _resident((1, HEAD_DIM))]
    args = [x, mod, g.reshape(1, d), w_qkv, q_g.reshape(1, HEAD_DIM), k_g.reshape(1, HEAD_DIM)]
    if rope:
        in_specs += [pl.BlockSpec((tm, HEAD_DIM), lambda b, j: (j, 0))] * 3
        args += list(rope_tabs)
    return pl.pallas_call(
        functools.partial(_qkv_kernel, rope=rope, q_scale=q_scale),
        grid=(groups, t // tm),
        in_specs=in_specs,
        out_specs=[_row_spec(tm), _row_spec(tm, KV_DIM), _row_spec(tm, KV_DIM)],
        out_shape=[jax.ShapeDtypeStruct((groups, t, d), jnp.bfloat16),
                   jax.ShapeDtypeStruct((groups, t, KV_DIM), jnp.bfloat16),
                   jax.ShapeDtypeStruct((groups, t, KV_DIM), jnp.bfloat16)],
        compiler_params=_params(2),
        name="qkv_proj",
    )(*args)


def _attn_kernel(q_ref, k_ref, v_ref, o_ref):
    tq = q_ref.shape[0]
    q = jnp.concatenate([q_ref[:, g * HEAD_DIM:(g + 1) * HEAD_DIM] for g in range(GQA_GROUP)],
                        axis=0)
    s = lax.dot_general(q, k_ref[...], (((1,), (1,)), ((), ())),
                        preferred_element_type=jnp.float32)
    m = jnp.max(s, axis=-1, keepdims=True)
    p = jnp.exp2(s - m)
    l = jnp.sum(p, axis=-1, keepdims=True)
    o = _dot(p.astype(jnp.bfloat16), v_ref[...]) / l
    for g in range(GQA_GROUP):
        o_ref[:, g * HEAD_DIM:(g + 1) * HEAD_DIM] = o[g * tq:(g + 1) * tq].astype(o_ref.dtype)


def _attention(q, k, v, *, tq):
    b, n, d = q.shape
    kv_len = k.shape[1]
    gw = GQA_GROUP * HEAD_DIM
    return pl.pallas_call(
        _attn_kernel,
        grid=(b, N_KV_HEADS, n // tq),
        in_specs=[
            pl.BlockSpec((None, tq, gw), lambda bi, kh, j: (bi, j, kh)),
            pl.BlockSpec((None, kv_len, HEAD_DIM), lambda bi, kh, j: (bi, 0, kh)),
            pl.BlockSpec((None, kv_len, HEAD_DIM), lambda bi, kh, j: (bi, 0, kh)),
        ],
        out_specs=pl.BlockSpec((None, tq, gw), lambda bi, kh, j: (bi, j, kh)),
        out_shape=jax.ShapeDtypeStruct((b, n, d), jnp.bfloat16),
        compiler_params=_params(3),
        name="attention",
    )(q, k, v)


def _oproj_kernel(x_ref, o_ref, mod_ref, w_ref, out_ref):
    out_ref[...] = x_ref[...] + mod_ref[2:3, :] * _dot(o_ref[...], w_ref[...])


def _out_proj(x, o, mod, w_o, *, tm):
    groups, t, d = x.shape
    return pl.pallas_call(
        _oproj_kernel,
        grid=(groups, t // tm),
        in_specs=[_row_spec(tm), _row_spec(tm), _mod_spec(mod.shape[0] > 1), _resident(w_o.shape)],
        out_specs=_row_spec(tm),
        out_shape=jax.ShapeDtypeStruct(x.shape, jnp.float32),
        compiler_params=_params(2),
        name="out_proj",
    )(x, o, mod, w_o)


def _rope_tables(n):
    pos = jnp.arange(n, dtype=jnp.int32)
    row = (pos // GRID_W).astype(jnp.float32)
    col = (pos % GRID_W).astype(jnp.float32)
    inv_freq = ROPE_THETA ** (-jnp.arange(ROPE_FREQS, dtype=jnp.float32) / ROPE_FREQS)
    ang_r = row[:, None] * inv_freq
    ang_c = col[:, None] * inv_freq
    zeros = jnp.zeros_like(ang_r)
    cos = jnp.concatenate([jnp.cos(ang_r), jnp.cos(ang_r), jnp.cos(ang_c), jnp.cos(ang_c)], axis=1)
    sin_lo = jnp.concatenate([-jnp.sin(ang_r), zeros, -jnp.sin(ang_c), zeros], axis=1)
    sin_hi = jnp.concatenate([zeros, jnp.sin(ang_r), zeros, jnp.sin(ang_c)], axis=1)
    return cos, sin_lo, sin_hi


def kernel(x, c, ctx, c_ctx, ada_w, ada_b, norm_g, final_g, ffn_w_in, ffn_w_out,
           conv_w_in, conv_w, conv_w_out, attn_w_qkv, attn_q_g, attn_k_g, attn_w_o):
    b, n, d = x.shape
    n_ctx = ctx.shape[1]
    bf = jnp.bfloat16
    ffn_w_in_b, ffn_w_out_b = ffn_w_in.astype(bf), ffn_w_out.astype(bf)
    conv_w_in_b, conv_w_out_b = conv_w_in.astype(bf), conv_w_out.astype(bf)
    attn_w_qkv_b, attn_w_o_b = attn_w_qkv.astype(bf), attn_w_o.astype(bf)

    cvec = jnp.zeros((MOD_ROWS, d), jnp.float32).at[:b].set(c).at[b].set(c_ctx)
    mods = _ada_mods(cvec, ada_w, ada_b).reshape(DEPTH, MOD_ROWS, N_SUBLAYERS, N_MOD, d)
    rope_tabs = _rope_tables(n)
    q_scale = HEAD_DIM ** -0.5 * math.log2(math.e)

    tm_lat, tm_ctx, tq = 512, 256, 128
    x_lat, x_ctx = x, ctx
    for i in range(DEPTH):
        is_attn = (i % N_MIXERS) == 1
        last = i == DEPTH - 1
        ctx_feeds_mixer = (not last) or is_attn
        j = i // N_MIXERS
        m_lat = [mods[i, :b, s] for s in range(N_SUBLAYERS)]
        m_ctx = [mods[i, b:b + 1, s] for s in range(N_SUBLAYERS)]
        x_lat = _ffn(x_lat, m_lat[0], norm_g[i, 0], ffn_w_in_b[i, 0], ffn_w_out_b[i, 0], tm=tm_lat)
        if ctx_feeds_mixer:
            x_ctx = _ffn(x_ctx, m_ctx[0], norm_g[i, 0], ffn_w_in_b[i, 0], ffn_w_out_b[i, 0],
                         tm=tm_ctx)
        if is_attn:
            q_l, k_l, v_l = _qkv_proj(x_lat, m_lat[1], norm_g[i, 1], attn_w_qkv_b[j], attn_q_g[j],
                                      attn_k_g[j], rope_tabs, tm=tm_lat, q_scale=q_scale)
            q_c, k_c, v_c = _qkv_proj(x_ctx, m_ctx[1], norm_g[i, 1], attn_w_qkv_b[j], attn_q_g[j],
                                      attn_k_g[j], None, tm=tm_ctx, q_scale=q_scale)
            k_all = jnp.concatenate([k_c, k_l], axis=1)
            v_all = jnp.concatenate([v_c, v_l], axis=1)
            o_l = _attention(q_l, k_all, v_all, tq=tq)
            x_lat = _out_proj(x_lat, o_l, m_lat[1], attn_w_o_b[j], tm=tm_lat)
            if not last:
                o_c = _attention(q_c, k_c, v_c, tq=tq)
                x_ctx = _out_proj(x_ctx, o_c, m_ctx[1], attn_w_o_b[j], tm=tm_ctx)
        else:
            x_lat = _conv_mixer(x_lat, m_lat[1], norm_g[i, 1], conv_w_in_b[j], conv_w[j],
                                conv_w_out_b[j], tm=tm_lat)
            if not last:
                x_ctx = _conv_mixer(x_ctx, m_ctx[1], norm_g[i, 1], conv_w_in_b[j], conv_w[j],
                                    conv_w_out_b[j], tm=tm_ctx)
        x_lat = _ffn(x_lat, m_lat[2], norm_g[i, 2], ffn_w_in_b[i, 1], ffn_w_out_b[i, 1], tm=tm_lat,
                     final_g=final_g if last else None)
        if not last:
            x_ctx = _ffn(x_ctx, m_ctx[2], norm_g[i, 2], ffn_w_in_b[i, 1], ffn_w_out_b[i, 1],
                         tm=tm_ctx)
    return x_lat
```

```python
import functools
import math

import jax
import jax.numpy as jnp
from jax import lax
from jax.experimental import pallas as pl
from jax.experimental.pallas import tpu as pltpu

D_MODEL = 1024
DEPTH = 4
GRID_W = 64
N_MIXERS = 2
HEAD_DIM = 128
N_HEADS = D_MODEL // HEAD_DIM
N_KV_HEADS = N_HEADS // 4
GQA_GROUP = N_HEADS // N_KV_HEADS
KV_DIM = N_KV_HEADS * HEAD_DIM
QKV_DIM = (N_HEADS + 2 * N_KV_HEADS) * HEAD_DIM
ROPE_THETA = 10000.0
ROPE_FREQS = HEAD_DIM // 4
D_FF = 2816
N_SUBLAYERS = 3
N_MOD = 3
NORM_EPS = 1e-6
MACARON_WEIGHT = 0.5

VMEM_LIMIT_BYTES = 56 * 1024 * 1024
MOD_ROWS = 16
ADA_COL_BLOCK = 1024
FFN_CHUNKS = ((0, 1024), (1024, 2048), (2048, D_FF))


def _params(n_grid_dims):
    return pltpu.CompilerParams(
        dimension_semantics=("arbitrary",) * n_grid_dims,
        vmem_limit_bytes=VMEM_LIMIT_BYTES)


def _resident(shape):
    nd = len(shape)
    return pl.BlockSpec(shape, lambda *_: (0,) * nd, pipeline_mode=pl.Buffered(1))


def _sigmoid(x):
    return 1.0 / (1.0 + jnp.exp(-x))


def _norm_modulate(x, g, mod_ref):
    ms = jnp.mean(x * x, axis=-1, keepdims=True)
    a = g * (1.0 + mod_ref[1:2, :])
    h = (x * lax.rsqrt(ms + NORM_EPS)) * a + mod_ref[0:1, :]
    return h.astype(jnp.bfloat16)


def _dot(a, b):
    return jnp.dot(a, b, preferred_element_type=jnp.float32)


def _ada_kernel(c_ref, w_ref, b_ref, o_ref):
    c = c_ref[...]
    s = (c * _sigmoid(c)).astype(jnp.bfloat16)
    o_ref[...] = _dot(s, w_ref[...].astype(jnp.bfloat16)) + b_ref[...]


def _ada_mods(cvec, ada_w, ada_b):
    depth, d, n = ada_w.shape
    return pl.pallas_call(
        _ada_kernel,
        grid=(depth, n // ADA_COL_BLOCK),
        in_specs=[
            pl.BlockSpec((MOD_ROWS, d), lambda i, j: (0, 0)),
            pl.BlockSpec((None, d, ADA_COL_BLOCK), lambda i, j: (i, 0, j)),
            pl.BlockSpec((None, 1, ADA_COL_BLOCK), lambda i, j: (i, 0, j)),
        ],
        out_specs=pl.BlockSpec((None, MOD_ROWS, ADA_COL_BLOCK), lambda i, j: (i, 0, j)),
        out_shape=jax.ShapeDtypeStruct((depth, MOD_ROWS, n), jnp.float32),
        compiler_params=_params(2),
        name="ada_mods",
    )(cvec, ada_w, ada_b.reshape(depth, 1, n))


def _mod_spec(per_group):
    if per_group:
        return pl.BlockSpec((None, N_MOD, D_MODEL), lambda b, j: (b, 0, 0))
    return pl.BlockSpec((None, N_MOD, D_MODEL), lambda b, j: (0, 0, 0))


def _row_spec(tm, width=D_MODEL):
    return pl.BlockSpec((None, tm, width), lambda b, j: (b, j, 0))


def _ffn_kernel(x_ref, mod_ref, g_ref, w_in_ref, w_out_ref, *rest, final_norm):
    if final_norm:
        fg_ref, o_ref = rest
    else:
        (o_ref,) = rest
    x = x_ref[...]
    h = _norm_modulate(x, g_ref[...], mod_ref)
    acc = None
    for lo, hi in FFN_CHUNKS:
        gate = _dot(h, w_in_ref[:, lo:hi])
        up = _dot(h, w_in_ref[:, D_FF + lo:D_FF + hi])
        act = (gate * _sigmoid(gate) * up).astype(jnp.bfloat16)
        part = _dot(act, w_out_ref[lo:hi, :])
        acc = part if acc is None else acc + part
    y = x + (MACARON_WEIGHT * mod_ref[2:3, :]) * acc
    if final_norm:
        ms = jnp.mean(y * y, axis=-1, keepdims=True)
        y = (y * lax.rsqrt(ms + NORM_EPS)) * fg_ref[...]
    o_ref[...] = y


def _ffn(x, mod, g, w_in, w_out, *, tm, final_g=None):
    groups, t, d = x.shape
    in_specs = [_row_spec(tm), _mod_spec(mod.shape[0] > 1), _resident((1, d)),
                _resident(w_in.shape), _resident(w_out.shape)]
    args = [x, mod, g.reshape(1, d), w_in, w_out]
    if final_g is not None:
        in_specs.append(_resident((1, d)))
        args.append(final_g.reshape(1, d))
    return pl.pallas_call(
        functools.partial(_ffn_kernel, final_norm=final_g is not None),
        grid=(groups, t // tm),
        in_specs=in_specs,
        out_specs=_row_spec(tm),
        out_shape=jax.ShapeDtypeStruct(x.shape, jnp.float32),
        compiler_params=_params(2),
        name="ffn",
    )(*args)


HALO = 8


def _conv_kernel(x_ref, xp_ref, xn_ref, mod_ref, g_ref, w_in_ref, cw_ref, w_out_ref, o_ref):
    j = pl.program_id(1)
    nj = pl.num_programs(1)
    d = D_MODEL
    g = g_ref[...]
    x = x_ref[...]
    tm = x.shape[0]
    h = _norm_modulate(x, g, mod_ref)
    bcv = _dot(h, w_in_ref[...])
    u = bcv[:, d:2 * d] * bcv[:, 2 * d:]
    hp = _norm_modulate(xp_ref[...], g, mod_ref)
    cvp = _dot(hp, w_in_ref[:, d:])
    up = (cvp[:, :d] * cvp[:, d:])[HALO - 1:HALO, :]
    up = jnp.where(j > 0, up, 0.0)
    hn = _norm_modulate(xn_ref[...], g, mod_ref)
    cvn = _dot(hn, w_in_ref[:, d:])
    un = (cvn[:, :d] * cvn[:, d:])[0:1, :]
    un = jnp.where(j < nj - 1, un, 0.0)
    row = lax.broadcasted_iota(jnp.int32, (tm, 1), 0)
    u_prev = jnp.where(row == 0, up, pltpu.roll(u, 1, 0))
    u_next = jnp.where(row == tm - 1, un, pltpu.roll(u, tm - 1, 0))
    y = cw_ref[0:1, :] * u_prev + cw_ref[1:2, :] * u + cw_ref[2:3, :] * u_next
    z = (bcv[:, :d] * y).astype(jnp.bfloat16)
    o_ref[...] = x + mod_ref[2:3, :] * _dot(z, w_out_ref[...])


def _conv_mixer(x, mod, g, w_in, conv_w, w_out, *, tm):
    groups, t, d = x.shape
    nj = t // tm
    hb = tm // HALO
    last_halo_block = t // HALO - 1
    prev_spec = pl.BlockSpec((None, HALO, d), lambda b, j: (b, jnp.maximum(j * hb - 1, 0), 0))
    next_spec = pl.BlockSpec((None, HALO, d),
                             lambda b, j: (b, jnp.minimum((j + 1) * hb, last_halo_block), 0))
    return pl.pallas_call(
        _conv_kernel,
        grid=(groups, nj),
        in_specs=[_row_spec(tm), prev_spec, next_spec, _mod_spec(mod.shape[0] > 1),
                  _resident((1, d)), _resident(w_in.shape), _resident(conv_w.shape),
                  _resident(w_out.shape)],
        out_specs=_row_spec(tm),
        out_shape=jax.ShapeDtypeStruct(x.shape, jnp.float32),
        compiler_params=_params(2),
        name="conv_mixer",
    )(x, x, x, mod, g.reshape(1, d), w_in, conv_w, w_out)


def _head_norm(xh, g):
    ms = jnp.mean(xh * xh, axis=-1, keepdims=True)
    return (xh * lax.rsqrt(ms + NORM_EPS)) * g


def _rope(xh, cos, sin_lo, sin_hi):
    return (xh * cos + pltpu.roll(xh, HEAD_DIM - ROPE_FREQS, 1) * sin_lo
            + pltpu.roll(xh, ROPE_FREQS, 1) * sin_hi)


def _qkv_kernel(x_ref, mod_ref, g_ref, w_ref, qg_ref, kg_ref, *rest, rope, q_scale):
    if rope:
        cos_ref, slo_ref, shi_ref, q_ref, k_ref, v_ref = rest
        cos, slo, shi = cos_ref[...], slo_ref[...], shi_ref[...]
    else:
        q_ref, k_ref, v_ref = rest
    h = _norm_modulate(x_ref[...], g_ref[...], mod_ref)
    qkv = _dot(h, w_ref[...])
    qg = qg_ref[...]
    kg = kg_ref[...]
    for hd in range(N_HEADS + N_KV_HEADS):
        sl = slice(hd * HEAD_DIM, (hd + 1) * HEAD_DIM)
        is_q = hd < N_HEADS
        xh = _head_norm(qkv[:, sl], qg if is_q else kg)
        if rope:
            xh = _rope(xh, cos, slo, shi)
        if is_q:
            q_ref[:, sl] = (xh * q_scale).astype(q_ref.dtype)
        else:
            ks = slice((hd - N_HEADS) * HEAD_DIM, (hd - N_HEADS + 1) * HEAD_DIM)
            k_ref[:, ks] = xh.astype(k_ref.dtype)
    v_ref[...] = qkv[:, D_MODEL + KV_DIM:].T.astype(v_ref.dtype)


def _qkv_proj(x, mod, g, w_qkv, q_g, k_g, rope_tabs, *, tm, q_scale):
    groups, t, d = x.shape
    rope = rope_tabs is not None
    in_specs = [_row_spec(tm), _mod_spec(mod.shape[0] > 1), _resident((1, d)),
                _resident(w_qkv.shape), _resident((1, HEAD_DIM)), _resident((1, HEAD_DIM))]
    args = [x, mod, g.reshape(1, d), w_qkv, q_g.reshape(1, HEAD_DIM), k_g.reshape(1, HEAD_DIM)]
    if rope:
        in_specs += [pl.BlockSpec((tm, HEAD_DIM), lambda b, j: (j, 0))] * 3
        args += list(rope_tabs)
    return pl.pallas_call(
        functools.partial(_qkv_kernel, rope=rope, q_scale=q_scale),
        grid=(groups, t // tm),
        in_specs=in_specs,
        out_specs=[_row_spec(tm), _row_spec(tm, KV_DIM),
                   pl.BlockSpec((None, KV_DIM, tm), lambda b, j: (b, 0, j))],
        out_shape=[jax.ShapeDtypeStruct((groups, t, d), jnp.bfloat16),
                   jax.ShapeDtypeStruct((groups, t, KV_DIM), jnp.bfloat16),
                   jax.ShapeDtypeStruct((groups, KV_DIM, t), jnp.bfloat16)],
        compiler_params=_params(2),
        name="qkv_proj",
    )(*args)


ATTN_KEY_CHUNK = 512


def _attn_kernel(shift_ref, q_ref, *rest, n_src, bounded):
    kv_refs, o_ref = rest[:2 * n_src], rest[2 * n_src]
    tq = q_ref.shape[0]
    q = jnp.concatenate([q_ref[:, g * HEAD_DIM:(g + 1) * HEAD_DIM] for g in range(GQA_GROUP)],
                        axis=0)
    chunks = []
    for src in range(n_src):
        k_ref, vt_ref = kv_refs[2 * src], kv_refs[2 * src + 1]
        length = k_ref.shape[0]
        tk = min(ATTN_KEY_CHUNK, length)
        chunks += [(k_ref.at[lo:lo + tk, :], vt_ref.at[:, lo:lo + tk]) for lo in range(0, length, tk)]

    def scores(c):
        return lax.dot_general(chunks[c][0][...], q, (((1,), (1,)), ((), ())),
                               preferred_element_type=jnp.float32)

    m = l = acc = None
    s_next = scores(0)
    for c, (_, vt_c) in enumerate(chunks):
        s = s_next
        if c + 1 < len(chunks):
            s_next = scores(c + 1)
        if bounded:
            p = jnp.exp2(s - shift_ref[0, 0])
            l_c = jnp.sum(p.reshape(p.shape[0] // 8, 8, p.shape[1]), axis=0)
            acc_c = _dot(vt_c[...], p.astype(jnp.bfloat16))
            l = l_c if l is None else l + l_c
            acc = acc_c if acc is None else acc + acc_c
            continue
        s_max = jnp.max(s, axis=0, keepdims=True)
        if m is None:
            m = s_max
            p = jnp.exp2(s - m)
            l = jnp.sum(p, axis=0, keepdims=True)
            acc = _dot(vt_c[...], p.astype(jnp.bfloat16))
        else:
            m_new = jnp.maximum(m, s_max)
            alpha = jnp.exp2(m - m_new)
            p = jnp.exp2(s - m_new)
            l = alpha * l + jnp.sum(p, axis=0, keepdims=True)
            acc = alpha * acc + _dot(vt_c[...], p.astype(jnp.bfloat16))
            m = m_new
    if bounded:
        l = jnp.sum(l, axis=0, keepdims=True)
    o = (acc / l).T
    for g in range(GQA_GROUP):
        o_ref[:, g * HEAD_DIM:(g + 1) * HEAD_DIM] = o[g * tq:(g + 1) * tq].astype(o_ref.dtype)


def _attention_call(shift, q, kv_sources, *, tq, bounded):
    b, n, d = q.shape
    gw = GQA_GROUP * HEAD_DIM
    in_specs = [pl.BlockSpec(memory_space=pltpu.SMEM),
                pl.BlockSpec((None, tq, gw), lambda bi, kh, j: (bi, j, kh))]
    args = [shift, q]
    for k, vt in kv_sources:
        length = k.shape[1]
        in_specs.append(pl.BlockSpec((None, length, HEAD_DIM), lambda bi, kh, j: (bi, 0, kh)))
        in_specs.append(pl.BlockSpec((None, HEAD_DIM, length), lambda bi, kh, j: (bi, kh, 0)))
        args += [k, vt]
    return pl.pallas_call(
        functools.partial(_attn_kernel, n_src=len(kv_sources), bounded=bounded),
        grid=(b, N_KV_HEADS, n // tq),
        in_specs=in_specs,
        out_specs=pl.BlockSpec((None, tq, gw), lambda bi, kh, j: (bi, j, kh)),
        out_shape=jax.ShapeDtypeStruct((b, n, d), jnp.bfloat16),
        compiler_params=_params(3),
        name="attention_bounded" if bounded else "attention_online",
    )(*args)


MAX_SAFE_SCORE_BOUND = 40.0


def _attention(q, kv_sources, score_bound, *, tq):
    return lax.cond(
        score_bound <= MAX_SAFE_SCORE_BOUND,
        functools.partial(_attention_call, tq=tq, bounded=True),
        functools.partial(_attention_call, tq=tq, bounded=False),
        score_bound.reshape(1, 1), q, kv_sources)


def _oproj_kernel(x_ref, o_ref, mod_ref, w_ref, out_ref):
    out_ref[...] = x_ref[...] + mod_ref[2:3, :] * _dot(o_ref[...], w_ref[...])


def _out_proj(x, o, mod, w_o, *, tm):
    groups, t, d = x.shape
    return pl.pallas_call(
        _oproj_kernel,
        grid=(groups, t // tm),
        in_specs=[_row_spec(tm), _row_spec(tm), _mod_spec(mod.shape[0] > 1), _resident(w_o.shape)],
        out_specs=_row_spec(tm),
        out_shape=jax.ShapeDtypeStruct(x.shape, jnp.float32),
        compiler_params=_params(2),
        name="out_proj",
    )(x, o, mod, w_o)


def _rope_tables(n):
    pos = jnp.arange(n, dtype=jnp.int32)
    row = (pos // GRID_W).astype(jnp.float32)
    col = (pos % GRID_W).astype(jnp.float32)
    inv_freq = ROPE_THETA ** (-jnp.arange(ROPE_FREQS, dtype=jnp.float32) / ROPE_FREQS)
    ang_r = row[:, None] * inv_freq
    ang_c = col[:, None] * inv_freq
    zeros = jnp.zeros_like(ang_r)
    cos = jnp.concatenate([jnp.cos(ang_r), jnp.cos(ang_r), jnp.cos(ang_c), jnp.cos(ang_c)], axis=1)
    sin_lo = jnp.concatenate([-jnp.sin(ang_r), zeros, -jnp.sin(ang_c), zeros], axis=1)
    sin_hi = jnp.concatenate([zeros, jnp.sin(ang_r), zeros, jnp.sin(ang_c)], axis=1)
    return cos, sin_lo, sin_hi


def kernel(x, c, ctx, c_ctx, ada_w, ada_b, norm_g, final_g, ffn_w_in, ffn_w_out,
           conv_w_in, conv_w, conv_w_out, attn_w_qkv, attn_q_g, attn_k_g, attn_w_o):
    b, n, d = x.shape
    n_ctx = ctx.shape[1]
    bf = jnp.bfloat16
    ffn_w_in_b, ffn_w_out_b = ffn_w_in.astype(bf), ffn_w_out.astype(bf)
    conv_w_in_b, conv_w_out_b = conv_w_in.astype(bf), conv_w_out.astype(bf)
    attn_w_qkv_b, attn_w_o_b = attn_w_qkv.astype(bf), attn_w_o.astype(bf)

    cvec = jnp.zeros((MOD_ROWS, d), jnp.float32).at[:b].set(c).at[b].set(c_ctx)
    mods = _ada_mods(cvec, ada_w, ada_b).reshape(DEPTH, MOD_ROWS, N_SUBLAYERS, N_MOD, d)
    rope_tabs = _rope_tables(n)
    q_scale = HEAD_DIM ** -0.5 * math.log2(math.e)

    tm_lat, tm_ctx, tq = 512, 256, 256
    x_lat, x_ctx = x, ctx
    for i in range(DEPTH):
        is_attn = (i % N_MIXERS) == 1
        last = i == DEPTH - 1
        ctx_feeds_mixer = (not last) or is_attn
        j = i // N_MIXERS
        m_lat = [mods[i, :b, s] for s in range(N_SUBLAYERS)]
        m_ctx = [mods[i, b:b + 1, s] for s in range(N_SUBLAYERS)]
        x_lat = _ffn(x_lat, m_lat[0], norm_g[i, 0], ffn_w_in_b[i, 0], ffn_w_out_b[i, 0], tm=tm_lat)
        if ctx_feeds_mixer:
            x_ctx = _ffn(x_ctx, m_ctx[0], norm_g[i, 0], ffn_w_in_b[i, 0], ffn_w_out_b[i, 0],
                         tm=tm_ctx)
        if is_attn:
            q_l, k_l, v_l = _qkv_proj(x_lat, m_lat[1], norm_g[i, 1], attn_w_qkv_b[j], attn_q_g[j],
                                      attn_k_g[j], rope_tabs, tm=tm_lat, q_scale=q_scale)
            q_c, k_c, v_c = _qkv_proj(x_ctx, m_ctx[1], norm_g[i, 1], attn_w_qkv_b[j], attn_q_g[j],
                                      attn_k_g[j], None, tm=tm_ctx, q_scale=q_scale)
            score_bound = (1.02 * HEAD_DIM * q_scale * jnp.max(jnp.abs(attn_q_g[j]))
                           * jnp.max(jnp.abs(attn_k_g[j])))
            o_l = _attention(q_l, [(k_c, v_c), (k_l, v_l)], score_bound, tq=tq)
            x_lat = _out_proj(x_lat, o_l, m_lat[1], attn_w_o_b[j], tm=tm_lat)
            if not last:
                o_c = _attention(q_c, [(k_c, v_c)], score_bound, tq=tq)
                x_ctx = _out_proj(x_ctx, o_c, m_ctx[1], attn_w_o_b[j], tm=tm_ctx)
        else:
            x_lat = _conv_mixer(x_lat, m_lat[1], norm_g[i, 1], conv_w_in_b[j], conv_w[j],
                                conv_w_out_b[j], tm=tm_lat)
            if not last:
                x_ctx = _conv_mixer(x_ctx, m_ctx[1], norm_g[i, 1], conv_w_in_b[j], conv_w[j],
                                    conv_w_out_b[j], tm=tm_ctx)
        x_lat = _ffn(x_lat, m_lat[2], norm_g[i, 2], ffn_w_in_b[i, 1], ffn_w_out_b[i, 1], tm=tm_lat,
                     final_g=final_g if last else None)
        if not last:
            x_ctx = _ffn(x_ctx, m_ctx[2], norm_g[i, 2], ffn_w_in_b[i, 1], ffn_w_out_b[i, 1],
                         tm=tm_ctx)
    return x_lat
```

```python
import functools
import math

import jax
import jax.numpy as jnp
from jax import lax
from jax.experimental import pallas as pl
from jax.experimental.pallas import tpu as pltpu

D_MODEL = 1024
DEPTH = 4
GRID_W = 64
N_MIXERS = 2
HEAD_DIM = 128
N_HEADS = D_MODEL // HEAD_DIM
N_KV_HEADS = N_HEADS // 4
GQA_GROUP = N_HEADS // N_KV_HEADS
KV_DIM = N_KV_HEADS * HEAD_DIM
QKV_DIM = (N_HEADS + 2 * N_KV_HEADS) * HEAD_DIM
ROPE_THETA = 10000.0
ROPE_FREQS = HEAD_DIM // 4
D_FF = 2816
N_SUBLAYERS = 3
N_MOD = 3
NORM_EPS = 1e-6
MACARON_WEIGHT = 0.5

VMEM_LIMIT_BYTES = 56 * 1024 * 1024
MOD_ROWS = 16
ADA_COL_BLOCK = 1024
FFN_CHUNKS = ((0, 1024), (1024, 2048), (2048, D_FF))
FFN_SUB_ROWS = 512


def _params(n_grid_dims):
    return pltpu.CompilerParams(
        dimension_semantics=("arbitrary",) * n_grid_dims,
        vmem_limit_bytes=VMEM_LIMIT_BYTES)


def _resident(shape, at=()):
    index = tuple(at) + (0,) * len(shape)
    return pl.BlockSpec((None,) * len(at) + tuple(shape), lambda *_: index,
                        pipeline_mode=pl.Buffered(1))


def _sigmoid(x):
    return 1.0 / (1.0 + jnp.exp(-x))


def _norm_modulate(x, g, mod_ref):
    ms = jnp.mean(x * x, axis=-1, keepdims=True)
    a = g * (1.0 + mod_ref[1:2, :])
    h = (x * lax.rsqrt(ms + NORM_EPS)) * a + mod_ref[0:1, :]
    return h.astype(jnp.bfloat16)


def _dot(a, b):
    return jnp.dot(a, b, preferred_element_type=jnp.float32)


def _ada_kernel(c_ref, w_ref, b_ref, o_ref):
    c = c_ref[...]
    s = (c * _sigmoid(c)).astype(jnp.bfloat16)
    o_ref[...] = _dot(s, w_ref[...].astype(jnp.bfloat16)) + b_ref[...]


def _ada_mods(cvec, ada_w, ada_b):
    depth, d, n = ada_w.shape
    return pl.pallas_call(
        _ada_kernel,
        grid=(depth, n // ADA_COL_BLOCK),
        in_specs=[
            pl.BlockSpec((MOD_ROWS, d), lambda i, j: (0, 0)),
            pl.BlockSpec((None, d, ADA_COL_BLOCK), lambda i, j: (i, 0, j)),
            pl.BlockSpec((None, 1, ADA_COL_BLOCK), lambda i, j: (i, 0, j)),
        ],
        out_specs=pl.BlockSpec((None, MOD_ROWS, ADA_COL_BLOCK), lambda i, j: (i, 0, j)),
        out_shape=jax.ShapeDtypeStruct((depth, MOD_ROWS, n), jnp.float32),
        compiler_params=_params(2),
        name="ada_mods",
    )(cvec, ada_w, ada_b.reshape(depth, 1, n))


def _mod_spec(per_group):
    if per_group:
        return pl.BlockSpec((None, N_MOD, D_MODEL), lambda b, j: (b, 0, 0))
    return pl.BlockSpec((None, N_MOD, D_MODEL), lambda b, j: (0, 0, 0))


def _row_spec(tm, width=D_MODEL):
    return pl.BlockSpec((None, tm, width), lambda b, j: (b, j, 0))


def _ffn_kernel(x_ref, mod_ref, g_ref, w_in_ref, w_out_ref, *rest, final_norm):
    if final_norm:
        fg_ref, o_ref = rest
    else:
        (o_ref,) = rest
    tm = x_ref.shape[0]
    sub = min(tm, FFN_SUB_ROWS)
    subs = [pl.ds(r, sub) for r in range(0, tm, sub)]
    hs = [_norm_modulate(x_ref[rows, :], g_ref[...], mod_ref) for rows in subs]
    for rows, h in zip(subs, hs):
        acc = None
        for lo, hi in FFN_CHUNKS:
            gate = _dot(h, w_in_ref[:, lo:hi])
            up = _dot(h, w_in_ref[:, D_FF + lo:D_FF + hi])
            act = (gate * _sigmoid(gate) * up).astype(jnp.bfloat16)
            part = _dot(act, w_out_ref[lo:hi, :])
            acc = part if acc is None else acc + part
        y = x_ref[rows, :] + (MACARON_WEIGHT * mod_ref[2:3, :]) * acc
        if final_norm:
            ms = jnp.mean(y * y, axis=-1, keepdims=True)
            y = (y * lax.rsqrt(ms + NORM_EPS)) * fg_ref[...]
        o_ref[rows, :] = y


def _ffn(x, mod, norm_g, w_in, w_out, layer, which, *, tm, final_g=None):
    groups, t, d = x.shape
    in_specs = [_row_spec(tm), _mod_spec(mod.shape[0] > 1),
                _resident((1, d), at=(layer * N_SUBLAYERS + 2 * which,)),
                _resident(w_in.shape[2:], at=(layer, which)),
                _resident(w_out.shape[2:], at=(layer, which))]
    args = [x, mod, norm_g, w_in, w_out]
    if final_g is not None:
        in_specs.append(_resident((1, d)))
        args.append(final_g.reshape(1, d))
    return pl.pallas_call(
        functools.partial(_ffn_kernel, final_norm=final_g is not None),
        grid=(groups, t // tm),
        in_specs=in_specs,
        out_specs=_row_spec(tm),
        out_shape=jax.ShapeDtypeStruct(x.shape, jnp.float32),
        compiler_params=_params(2),
        name="ffn",
    )(*args)


HALO = 8


CONV_CHUNKS = ((0, 512), (512, D_MODEL))


def _conv_kernel(x_ref, xp_ref, xn_ref, mod_ref, g_ref, w_in_ref, cw_ref, w_out_ref, o_ref):
    j = pl.program_id(1)
    nj = pl.num_programs(1)
    d = D_MODEL
    tm = x_ref.shape[0]
    n_ext = tm + 2 * HALO
    main = slice(HALO, HALO + tm)
    x_ext = jnp.concatenate([xp_ref[...], x_ref[...], xn_ref[...]], axis=0)
    h = _norm_modulate(x_ext, g_ref[...], mod_ref)
    row = lax.broadcasted_iota(jnp.int32, (n_ext, 1), 0)
    valid = ((row >= HALO) | (j > 0)) & ((row < HALO + tm) | (j < nj - 1))
    acc = None
    for lo, hi in CONV_CHUNKS:
        b_gate = _dot(h, w_in_ref[:, lo:hi])[main]
        u = _dot(h, w_in_ref[:, d + lo:d + hi]) * _dot(h, w_in_ref[:, 2 * d + lo:2 * d + hi])
        u = jnp.where(valid, u, 0.0)
        y = (cw_ref[0:1, lo:hi] * pltpu.roll(u, 1, 0)[main] + cw_ref[1:2, lo:hi] * u[main]
             + cw_ref[2:3, lo:hi] * pltpu.roll(u, n_ext - 1, 0)[main])
        part = _dot((b_gate * y).astype(jnp.bfloat16), w_out_ref[lo:hi, :])
        acc = part if acc is None else acc + part
    o_ref[...] = x_ref[...] + mod_ref[2:3, :] * acc


def _conv_mixer(x, mod, norm_g, w_in, conv_w, w_out, layer, which, *, tm):
    groups, t, d = x.shape
    nj = t // tm
    hb = tm // HALO
    last_halo_block = t // HALO - 1
    prev_spec = pl.BlockSpec((None, HALO, d), lambda b, j: (b, jnp.maximum(j * hb - 1, 0), 0))
    next_spec = pl.BlockSpec((None, HALO, d),
                             lambda b, j: (b, jnp.minimum((j + 1) * hb, last_halo_block), 0))
    return pl.pallas_call(
        _conv_kernel,
        grid=(groups, nj),
        in_specs=[_row_spec(tm), prev_spec, next_spec, _mod_spec(mod.shape[0] > 1),
                  _resident((1, d), at=(layer * N_SUBLAYERS + 1,)),
                  _resident(w_in.shape[1:], at=(which,)), _resident(conv_w.shape[1:], at=(which,)),
                  _resident(w_out.shape[1:], at=(which,))],
        out_specs=_row_spec(tm),
        out_shape=jax.ShapeDtypeStruct(x.shape, jnp.float32),
        compiler_params=_params(2),
        name="conv_mixer",
    )(x, x, x, mod, norm_g, w_in, conv_w, w_out)


HEAD_PERM = tuple(list(range(0, 32)) + list(range(64, 96)) + list(range(32, 64))
                  + list(range(96, 128)))
LANES = 128


def _qkv_kernel(x_ref, mod_ref, g_ref, wt_ref, qg_ref, kg_ref, *rest, rope):
    if rope:
        cos_ref, sin_ref, q_ref, k_ref, v_ref = rest
        cos, sin = cos_ref[...], sin_ref[...]
    else:
        q_ref, k_ref, v_ref = rest
    tm = x_ref.shape[0]
    h = _norm_modulate(x_ref[...], g_ref[...], mod_ref)
    yt = lax.dot_general(wt_ref[...], h, (((1,), (1,)), ((), ())),
                         preferred_element_type=jnp.float32)
    qg = jnp.tile(qg_ref[...], (1, tm // LANES))
    kg = jnp.tile(kg_ref[...], (1, tm // LANES))
    half = HEAD_DIM // 2
    for hd in range(N_HEADS + N_KV_HEADS):
        xh = yt[hd * HEAD_DIM:(hd + 1) * HEAD_DIM, :]
        is_q = hd < N_HEADS
        ms = jnp.mean(xh * xh, axis=0, keepdims=True)
        xh = (xh * lax.rsqrt(ms + NORM_EPS)) * (qg if is_q else kg)
        if rope:
            x1, x2 = xh[:half], xh[half:]
            xh = jnp.concatenate([x1 * cos - x2 * sin, x2 * cos + x1 * sin], axis=0)
        if is_q:
            q_ref[hd * HEAD_DIM:(hd + 1) * HEAD_DIM, :] = xh.astype(q_ref.dtype)
        else:
            kd = hd - N_HEADS
            k_ref[:, kd * HEAD_DIM:(kd + 1) * HEAD_DIM] = xh.T.astype(k_ref.dtype)
    v_ref[...] = yt[D_MODEL + KV_DIM:, :].astype(v_ref.dtype)


def _qkv_proj(x, mod, norm_g, wt_qkv, q_g, k_g, rope_tabs, layer, which, *, tm):
    groups, t, d = x.shape
    rope = rope_tabs is not None
    in_specs = [_row_spec(tm), _mod_spec(mod.shape[0] > 1),
                _resident((1, d), at=(layer * N_SUBLAYERS + 1,)),
                _resident(wt_qkv.shape[1:], at=(which,)),
                _resident((HEAD_DIM, LANES), at=(which,)), _resident((HEAD_DIM, LANES), at=(which,))]
    args = [x, mod, norm_g, wt_qkv, q_g, k_g]
    if rope:
        in_specs += [pl.BlockSpec((HEAD_DIM // 2, tm), lambda b, j: (0, j))] * 2
        args += list(rope_tabs)
    return pl.pallas_call(
        functools.partial(_qkv_kernel, rope=rope),
        grid=(groups, t // tm),
        in_specs=in_specs,
        out_specs=[pl.BlockSpec((None, d, tm), lambda b, j: (b, 0, j)), _row_spec(tm, KV_DIM),
                   pl.BlockSpec((None, KV_DIM, tm), lambda b, j: (b, 0, j))],
        out_shape=[jax.ShapeDtypeStruct((groups, d, t), jnp.bfloat16),
                   jax.ShapeDtypeStruct((groups, t, KV_DIM), jnp.bfloat16),
                   jax.ShapeDtypeStruct((groups, KV_DIM, t), jnp.bfloat16)],
        compiler_params=_params(2),
        name="qkv_proj",
    )(*args)


ATTN_KEY_CHUNK = 512


def _attn_kernel(shift_ref, q_ref, *rest, n_src, bounded):
    kv_refs, o_ref = rest[:2 * n_src], rest[2 * n_src]
    tq = q_ref.shape[1]
    qt = jnp.concatenate([q_ref[g * HEAD_DIM:(g + 1) * HEAD_DIM, :] for g in range(GQA_GROUP)],
                         axis=1)
    chunks = []
    for src in range(n_src):
        k_ref, vt_ref = kv_refs[2 * src], kv_refs[2 * src + 1]
        length = k_ref.shape[0]
        tk = min(ATTN_KEY_CHUNK, length)
        chunks += [(k_ref.at[lo:lo + tk, :], vt_ref.at[:, lo:lo + tk]) for lo in range(0, length, tk)]

    def scores(c):
        return _dot(chunks[c][0][...], qt)

    m = l = acc = None
    s_next = scores(0)
    for c, (_, vt_c) in enumerate(chunks):
        s = s_next
        if c + 1 < len(chunks):
            s_next = scores(c + 1)
        if bounded:
            p = jnp.exp2(s - shift_ref[0, 0])
            l_c = jnp.sum(p.reshape(p.shape[0] // 8, 8, p.shape[1]), axis=0)
            acc_c = _dot(vt_c[...], p.astype(jnp.bfloat16))
            l = l_c if l is None else l + l_c
            acc = acc_c if acc is None else acc + acc_c
            continue
        s_max = jnp.max(s, axis=0, keepdims=True)
        if m is None:
            m = s_max
            p = jnp.exp2(s - m)
            l = jnp.sum(p, axis=0, keepdims=True)
            acc = _dot(vt_c[...], p.astype(jnp.bfloat16))
        else:
            m_new = jnp.maximum(m, s_max)
            alpha = jnp.exp2(m - m_new)
            p = jnp.exp2(s - m_new)
            l = alpha * l + jnp.sum(p, axis=0, keepdims=True)
            acc = alpha * acc + _dot(vt_c[...], p.astype(jnp.bfloat16))
            m = m_new
    if bounded:
        l = jnp.sum(l, axis=0, keepdims=True)
    o = (acc / l).T
    for g in range(GQA_GROUP):
        o_ref[:, g * HEAD_DIM:(g + 1) * HEAD_DIM] = o[g * tq:(g + 1) * tq].astype(o_ref.dtype)


def _attention_call(shift, qt, kv_sources, *, tq, bounded):
    b, d, n = qt.shape
    gw = GQA_GROUP * HEAD_DIM
    in_specs = [pl.BlockSpec(memory_space=pltpu.SMEM),
                pl.BlockSpec((None, gw, tq), lambda bi, kh, j: (bi, kh, j))]
    args = [shift, qt]
    for k, vt in kv_sources:
        length = k.shape[1]
        in_specs.append(pl.BlockSpec((None, length, HEAD_DIM), lambda bi, kh, j: (bi, 0, kh)))
        in_specs.append(pl.BlockSpec((None, HEAD_DIM, length), lambda bi, kh, j: (bi, kh, 0)))
        args += [k, vt]
    return pl.pallas_call(
        functools.partial(_attn_kernel, n_src=len(kv_sources), bounded=bounded),
        grid=(b, N_KV_HEADS, n // tq),
        in_specs=in_specs,
        out_specs=pl.BlockSpec((None, tq, gw), lambda bi, kh, j: (bi, j, kh)),
        out_shape=jax.ShapeDtypeStruct((b, n, d), jnp.bfloat16),
        compiler_params=_params(3),
        name="attention_bounded" if bounded else "attention_online",
    )(*args)


MAX_SAFE_SCORE_BOUND = 40.0


def _attention(q, kv_sources, score_bound, *, tq):
    return lax.cond(
        score_bound <= MAX_SAFE_SCORE_BOUND,
        functools.partial(_attention_call, tq=tq, bounded=True),
        functools.partial(_attention_call, tq=tq, bounded=False),
        score_bound.reshape(1, 1), q, kv_sources)


def _oproj_kernel(x_ref, o_ref, mod_ref, w_ref, out_ref):
    out_ref[...] = x_ref[...] + mod_ref[2:3, :] * _dot(o_ref[...], w_ref[...])


def _out_proj(x, o, mod, w_o, which, *, tm):
    groups, t, d = x.shape
    return pl.pallas_call(
        _oproj_kernel,
        grid=(groups, t // tm),
        in_specs=[_row_spec(tm), _row_spec(tm), _mod_spec(mod.shape[0] > 1),
                  _resident(w_o.shape[1:], at=(which,))],
        out_specs=_row_spec(tm),
        out_shape=jax.ShapeDtypeStruct(x.shape, jnp.float32),
        compiler_params=_params(2),
        name="out_proj",
    )(x, o, mod, w_o)


def _rope_tables(n):
    pos = jnp.arange(n, dtype=jnp.int32)
    row = (pos // GRID_W).astype(jnp.float32)
    col = (pos % GRID_W).astype(jnp.float32)
    inv_freq = ROPE_THETA ** (-jnp.arange(ROPE_FREQS, dtype=jnp.float32) / ROPE_FREQS)
    ang = jnp.concatenate([inv_freq[:, None] * row[None, :], inv_freq[:, None] * col[None, :]],
                          axis=0)
    return jnp.cos(ang), jnp.sin(ang)


def _tiles(n, n_ctx_total):
    tm_lat = 1024 if n % 1024 == 0 else n
    tm_ctx_ffn = 1024 if n_ctx_total % 1024 == 0 else n_ctx_total
    tm_qkv = 512 if n % 512 == 0 else n
    tq = 256 if n % 256 == 0 else n
    return tm_lat, tm_ctx_ffn, tm_qkv, tq


def kernel(x, c, ctx, c_ctx, ada_w, ada_b, norm_g, final_g, ffn_w_in, ffn_w_out,
           conv_w_in, conv_w, conv_w_out, attn_w_qkv, attn_q_g, attn_k_g, attn_w_o):
    b, n, d = x.shape
    n_ctx = ctx.shape[1]
    bf = jnp.bfloat16
    ffn_w_in_b, ffn_w_out_b = ffn_w_in.astype(bf), ffn_w_out.astype(bf)
    conv_w_in_b, conv_w_out_b = conv_w_in.astype(bf), conv_w_out.astype(bf)
    attn_w_o_b = attn_w_o.astype(bf)
    head_perm = jnp.asarray(HEAD_PERM, jnp.int32)
    qk_cols = (jnp.arange(N_HEADS + N_KV_HEADS, dtype=jnp.int32)[:, None] * HEAD_DIM
               + head_perm[None, :]).reshape(-1)
    cols = jnp.concatenate([qk_cols, jnp.arange(D_MODEL + KV_DIM, QKV_DIM, dtype=jnp.int32)])
    attn_wt_qkv_b = attn_w_qkv[:, :, cols].transpose(0, 2, 1).astype(bf)
    q_scale = HEAD_DIM ** -0.5 * math.log2(math.e)
    n_attn = attn_q_g.shape[0]
    q_g = jnp.broadcast_to((attn_q_g[:, head_perm] * q_scale)[:, :, None],
                           (n_attn, HEAD_DIM, LANES))
    k_g = jnp.broadcast_to(attn_k_g[:, head_perm][:, :, None], (n_attn, HEAD_DIM, LANES))
    norm_g_rows = norm_g.reshape(DEPTH * N_SUBLAYERS, 1, d)

    cvec = jnp.zeros((MOD_ROWS, d), jnp.float32).at[:b].set(c).at[b].set(c_ctx)
    mods = _ada_mods(cvec, ada_w, ada_b).reshape(DEPTH, MOD_ROWS, N_SUBLAYERS, N_MOD, d)
    rope_tabs = _rope_tables(n)

    tm_lat, tm_ctx_ffn, tm_qkv, tq = _tiles(n, b * n_ctx)
    tq_ctx = min(tq, n_ctx)
    x_lat, x_ctx = x, ctx

    def ffn_ctx(x_c, mod, layer, which):
        y = _ffn(x_c.reshape(1, b * n_ctx, d), mod, norm_g_rows, ffn_w_in_b, ffn_w_out_b,
                 layer, which, tm=tm_ctx_ffn)
        return y.reshape(b, n_ctx, d)

    for i in range(DEPTH):
        is_attn = (i % N_MIXERS) == 1
        last = i == DEPTH - 1
        ctx_feeds_mixer = (not last) or is_attn
        j = i // N_MIXERS
        m_lat = [mods[i, :b, s] for s in range(N_SUBLAYERS)]
        m_ctx = [mods[i, b:b + 1, s] for s in range(N_SUBLAYERS)]
        x_lat = _ffn(x_lat, m_lat[0], norm_g_rows, ffn_w_in_b, ffn_w_out_b, i, 0, tm=tm_lat)
        if ctx_feeds_mixer:
            x_ctx = ffn_ctx(x_ctx, m_ctx[0], i, 0)
        if is_attn:
            q_l, k_l, v_l = _qkv_proj(x_lat, m_lat[1], norm_g_rows, attn_wt_qkv_b, q_g, k_g,
                                      rope_tabs, i, j, tm=tm_qkv)
            q_c, k_c, v_c = _qkv_proj(x_ctx, m_ctx[1], norm_g_rows, attn_wt_qkv_b, q_g, k_g,
                                      None, i, j, tm=n_ctx)
            score_bound = (1.02 * HEAD_DIM * q_scale * jnp.max(jnp.abs(attn_q_g[j]))
                           * jnp.max(jnp.abs(attn_k_g[j])))
            o_l = _attention(q_l, [(k_c, v_c), (k_l, v_l)], score_bound, tq=tq)
            x_lat = _out_proj(x_lat, o_l, m_lat[1], attn_w_o_b, j, tm=tm_lat)
            if not last:
                o_c = _attention(q_c, [(k_c, v_c)], score_bound, tq=tq_ctx)
                x_ctx = _out_proj(x_ctx, o_c, m_ctx[1], attn_w_o_b, j, tm=n_ctx)
        else:
            x_lat = _conv_mixer(x_lat, m_lat[1], norm_g_rows, conv_w_in_b, conv_w, conv_w_out_b,
                                i, j, tm=tm_lat)
            if not last:
                x_ctx = _conv_mixer(x_ctx, m_ctx[1], norm_g_rows, conv_w_in_b, conv_w,
                                    conv_w_out_b, i, j, tm=n_ctx)
        x_lat = _ffn(x_lat, m_lat[2], norm_g_rows, ffn_w_in_b, ffn_w_out_b, i, 1, tm=tm_lat,
                     final_g=final_g if last else None)
        if not last:
            x_ctx = ffn_ctx(x_ctx, m_ctx[2], i, 1)
    return x_lat
```

```python
import functools
import math

import jax
import jax.numpy as jnp
from jax import lax
from jax.experimental import pallas as pl
from jax.experimental.pallas import tpu as pltpu

D_MODEL = 1024
DEPTH = 4
GRID_W = 64
N_MIXERS = 2
HEAD_DIM = 128
N_HEADS = D_MODEL // HEAD_DIM
N_KV_HEADS = N_HEADS // 4
GQA_GROUP = N_HEADS // N_KV_HEADS
KV_DIM = N_KV_HEADS * HEAD_DIM
QKV_DIM = (N_HEADS + 2 * N_KV_HEADS) * HEAD_DIM
ROPE_THETA = 10000.0
ROPE_FREQS = HEAD_DIM // 4
D_FF = 2816
N_SUBLAYERS = 3
N_MOD = 3
NORM_EPS = 1e-6
MACARON_WEIGHT = 0.5

VMEM_LIMIT_BYTES = 56 * 1024 * 1024
MOD_ROWS = 16
ADA_COL_BLOCK = 1024
FFN_CHUNKS = ((0, 1024), (1024, 2048), (2048, D_FF))
FFN_SUB_ROWS = 512


def _params(n_grid_dims):
    return pltpu.CompilerParams(
        dimension_semantics=("arbitrary",) * n_grid_dims,
        vmem_limit_bytes=VMEM_LIMIT_BYTES)


def _resident(shape, at=()):
    index = tuple(at) + (0,) * len(shape)
    return pl.BlockSpec((None,) * len(at) + tuple(shape), lambda *_: index,
                        pipeline_mode=pl.Buffered(1))


def _sigmoid(x):
    return 1.0 / (1.0 + jnp.exp(-x))


def _norm_modulate(x, g, mod_ref):
    ms = jnp.mean(x * x, axis=-1, keepdims=True)
    a = g * (1.0 + mod_ref[1:2, :])
    h = (x * lax.rsqrt(ms + NORM_EPS)) * a + mod_ref[0:1, :]
    return h.astype(jnp.bfloat16)


def _dot(a, b):
    return jnp.dot(a, b, preferred_element_type=jnp.float32)


def _ada_kernel(c_ref, w_ref, b_ref, o_ref):
    c = c_ref[...]
    s = (c * _sigmoid(c)).astype(jnp.bfloat16)
    o_ref[...] = _dot(s, w_ref[...].astype(jnp.bfloat16)) + b_ref[...]


def _ada_mods(cvec, ada_w, ada_b):
    depth, d, n = ada_w.shape
    return pl.pallas_call(
        _ada_kernel,
        grid=(depth, n // ADA_COL_BLOCK),
        in_specs=[
            pl.BlockSpec((MOD_ROWS, d), lambda i, j: (0, 0)),
            pl.BlockSpec((None, d, ADA_COL_BLOCK), lambda i, j: (i, 0, j)),
            pl.BlockSpec((None, 1, ADA_COL_BLOCK), lambda i, j: (i, 0, j)),
        ],
        out_specs=pl.BlockSpec((None, MOD_ROWS, ADA_COL_BLOCK), lambda i, j: (i, 0, j)),
        out_shape=jax.ShapeDtypeStruct((depth, MOD_ROWS, n), jnp.float32),
        compiler_params=_params(2),
        name="ada_mods",
    )(cvec, ada_w, ada_b.reshape(depth, 1, n))


def _mod_spec(per_group):
    if per_group:
        return pl.BlockSpec((None, N_MOD, D_MODEL), lambda b, j: (b, 0, 0))
    return pl.BlockSpec((None, N_MOD, D_MODEL), lambda b, j: (0, 0, 0))


def _row_spec(tm, width=D_MODEL):
    return pl.BlockSpec((None, tm, width), lambda b, j: (b, j, 0))


def _ffn_kernel(x_ref, mod_ref, g_ref, w_in_ref, w_out_ref, *rest, attn_proj, final_norm):
    rest = list(rest)
    if attn_proj:
        ao_ref, mix_mod_ref, wo_ref = rest[:3]
        rest = rest[3:]
    if final_norm:
        fg_ref = rest.pop(0)
    (o_ref,) = rest
    tm = x_ref.shape[0]
    sub = min(tm, FFN_SUB_ROWS)
    subs = [pl.ds(r, sub) for r in range(0, tm, sub)]
    if attn_proj:
        for rows in subs:
            o_ref[rows, :] = x_ref[rows, :] + mix_mod_ref[2:3, :] * _dot(ao_ref[rows, :], wo_ref[...])
        res_ref = o_ref
    else:
        res_ref = x_ref
    hs = [_norm_modulate(res_ref[rows, :], g_ref[...], mod_ref) for rows in subs]
    for rows, h in zip(subs, hs):
        acc = None
        for lo, hi in FFN_CHUNKS:
            gate = _dot(h, w_in_ref[:, lo:hi])
            up = _dot(h, w_in_ref[:, D_FF + lo:D_FF + hi])
            act = (gate * _sigmoid(gate) * up).astype(jnp.bfloat16)
            part = _dot(act, w_out_ref[lo:hi, :])
            acc = part if acc is None else acc + part
        y = res_ref[rows, :] + (MACARON_WEIGHT * mod_ref[2:3, :]) * acc
        if final_norm:
            ms = jnp.mean(y * y, axis=-1, keepdims=True)
            y = (y * lax.rsqrt(ms + NORM_EPS)) * fg_ref[...]
        o_ref[rows, :] = y


def _ffn(x, mod, norm_g, w_in, w_out, layer, which, *, tm, attn=None, final_g=None):
    groups, t, d = x.shape
    in_specs = [_row_spec(tm), _mod_spec(mod.shape[0] > 1),
                _resident((1, d), at=(layer * N_SUBLAYERS + 2 * which,)),
                _resident(w_in.shape[2:], at=(layer, which)),
                _resident(w_out.shape[2:], at=(layer, which))]
    args = [x, mod, norm_g, w_in, w_out]
    if attn is not None:
        attn_out, mix_mod, w_o, attn_which = attn
        in_specs += [_row_spec(tm), _mod_spec(mix_mod.shape[0] > 1),
                     _resident(w_o.shape[1:], at=(attn_which,))]
        args += [attn_out, mix_mod, w_o]
    if final_g is not None:
        in_specs.append(_resident((1, d)))
        args.append(final_g.reshape(1, d))
    return pl.pallas_call(
        functools.partial(_ffn_kernel, attn_proj=attn is not None, final_norm=final_g is not None),
        grid=(groups, t // tm),
        in_specs=in_specs,
        out_specs=_row_spec(tm),
        out_shape=jax.ShapeDtypeStruct(x.shape, jnp.float32),
        compiler_params=_params(2),
        name="ffn",
    )(*args)


HALO = 8


CONV_CHUNKS = ((0, 512), (512, D_MODEL))


def _conv_kernel(x_ref, xp_ref, xn_ref, mod_ref, g_ref, w_in_ref, cw_ref, w_out_ref, o_ref):
    j = pl.program_id(1)
    nj = pl.num_programs(1)
    d = D_MODEL
    tm = x_ref.shape[0]
    n_ext = tm + 2 * HALO
    main = slice(HALO, HALO + tm)
    x_ext = jnp.concatenate([xp_ref[...], x_ref[...], xn_ref[...]], axis=0)
    h = _norm_modulate(x_ext, g_ref[...], mod_ref)
    row = lax.broadcasted_iota(jnp.int32, (n_ext, 1), 0)
    valid = ((row >= HALO) | (j > 0)) & ((row < HALO + tm) | (j < nj - 1))
    acc = None
    for lo, hi in CONV_CHUNKS:
        b_gate = _dot(h, w_in_ref[:, lo:hi])[main]
        u = _dot(h, w_in_ref[:, d + lo:d + hi]) * _dot(h, w_in_ref[:, 2 * d + lo:2 * d + hi])
        u = jnp.where(valid, u, 0.0)
        y = (cw_ref[0:1, lo:hi] * pltpu.roll(u, 1, 0)[main] + cw_ref[1:2, lo:hi] * u[main]
             + cw_ref[2:3, lo:hi] * pltpu.roll(u, n_ext - 1, 0)[main])
        part = _dot((b_gate * y).astype(jnp.bfloat16), w_out_ref[lo:hi, :])
        acc = part if acc is None else acc + part
    o_ref[...] = x_ref[...] + mod_ref[2:3, :] * acc


def _conv_mixer(x, mod, norm_g, w_in, conv_w, w_out, layer, which, *, tm):
    groups, t, d = x.shape
    nj = t // tm
    hb = tm // HALO
    last_halo_block = t // HALO - 1
    prev_spec = pl.BlockSpec((None, HALO, d), lambda b, j: (b, jnp.maximum(j * hb - 1, 0), 0))
    next_spec = pl.BlockSpec((None, HALO, d),
                             lambda b, j: (b, jnp.minimum((j + 1) * hb, last_halo_block), 0))
    return pl.pallas_call(
        _conv_kernel,
        grid=(groups, nj),
        in_specs=[_row_spec(tm), prev_spec, next_spec, _mod_spec(mod.shape[0] > 1),
                  _resident((1, d), at=(layer * N_SUBLAYERS + 1,)),
                  _resident(w_in.shape[1:], at=(which,)), _resident(conv_w.shape[1:], at=(which,)),
                  _resident(w_out.shape[1:], at=(which,))],
        out_specs=_row_spec(tm),
        out_shape=jax.ShapeDtypeStruct(x.shape, jnp.float32),
        compiler_params=_params(2),
        name="conv_mixer",
    )(x, x, x, mod, norm_g, w_in, conv_w, w_out)


LANES = 128

def _qkv_kernel(x_ref, mod_ref, g_ref, wt_ref, qg_ref, kg_ref, *rest, rope):
    if rope:
        cos_ref, sin_ref, q_ref, k_ref, v_ref = rest
        cos, sin = cos_ref[...], sin_ref[...]
    else:
        q_ref, k_ref, v_ref = rest
    tm = x_ref.shape[0]
    h = _norm_modulate(x_ref[...], g_ref[...], mod_ref)
    yt = lax.dot_general(wt_ref[...], h, (((1,), (1,)), ((), ())),
                         preferred_element_type=jnp.float32)
    qg = jnp.tile(qg_ref[...], (1, tm // LANES))
    kg = jnp.tile(kg_ref[...], (1, tm // LANES))
    half = HEAD_DIM // 2
    for hd in range(N_HEADS + N_KV_HEADS):
        xh = yt[hd * HEAD_DIM:(hd + 1) * HEAD_DIM, :]
        is_q = hd < N_HEADS
        ms = jnp.mean(xh * xh, axis=0, keepdims=True)
        xh = (xh * lax.rsqrt(ms + NORM_EPS)) * (qg if is_q else kg)
        if rope:
            x1, x2 = xh[:half], xh[half:]
            xh = jnp.concatenate([x1 * cos - x2 * sin, x2 * cos + x1 * sin], axis=0)
        if is_q:
            q_ref[hd * HEAD_DIM:(hd + 1) * HEAD_DIM, :] = xh.astype(q_ref.dtype)
        else:
            kd = hd - N_HEADS
            k_ref[:, kd * HEAD_DIM:(kd + 1) * HEAD_DIM] = xh.T.astype(k_ref.dtype)
    v_ref[...] = yt[D_MODEL + KV_DIM:, :].astype(v_ref.dtype)


def _qkv_proj(x, mod, norm_g, wt_qkv, q_g, k_g, rope_tabs, layer, which, *, tm):
    groups, t, d = x.shape
    rope = rope_tabs is not None
    in_specs = [_row_spec(tm), _mod_spec(mod.shape[0] > 1),
                _resident((1, d), at=(layer * N_SUBLAYERS + 1,)),
                _resident(wt_qkv.shape[1:], at=(which,)),
                _resident((HEAD_DIM, LANES), at=(which,)), _resident((HEAD_DIM, LANES), at=(which,))]
    args = [x, mod, norm_g, wt_qkv, q_g, k_g]
    if rope:
        in_specs += [pl.BlockSpec((HEAD_DIM // 2, tm), lambda b, j: (0, j))] * 2
        args += list(rope_tabs)
    return pl.pallas_call(
        functools.partial(_qkv_kernel, rope=rope),
        grid=(groups, t // tm),
        in_specs=in_specs,
        out_specs=[pl.BlockSpec((None, d, tm), lambda b, j: (b, 0, j)), _row_spec(tm, KV_DIM),
                   pl.BlockSpec((None, KV_DIM, tm), lambda b, j: (b, 0, j))],
        out_shape=[jax.ShapeDtypeStruct((groups, d, t), jnp.bfloat16),
                   jax.ShapeDtypeStruct((groups, t, KV_DIM), jnp.bfloat16),
                   jax.ShapeDtypeStruct((groups, KV_DIM, t), jnp.bfloat16)],
        compiler_params=_params(2),
        name="qkv_proj",
    )(*args)


ATTN_KEY_CHUNK = 2048


def _attn_kernel(shift_ref, q_ref, *rest, n_src, bounded):
    kv_refs, o_ref = rest[:2 * n_src], rest[2 * n_src]
    tq = q_ref.shape[1]
    qt = jnp.concatenate([q_ref[g * HEAD_DIM:(g + 1) * HEAD_DIM, :] for g in range(GQA_GROUP)],
                         axis=1)
    chunks = []
    for src in range(n_src):
        k_ref, vt_ref = kv_refs[2 * src], kv_refs[2 * src + 1]
        length = k_ref.shape[0]
        tk = min(ATTN_KEY_CHUNK, length)
        chunks += [(k_ref.at[lo:lo + tk, :], vt_ref.at[:, lo:lo + tk]) for lo in range(0, length, tk)]

    def scores(c):
        return _dot(chunks[c][0][...], qt)

    m = l = acc = None
    s_next = scores(0)
    for c, (_, vt_c) in enumerate(chunks):
        s = s_next
        if c + 1 < len(chunks):
            s_next = scores(c + 1)
        if bounded:
            p = jnp.exp2(s - shift_ref[0, 0])
            l_c = jnp.sum(p.reshape(p.shape[0] // 8, 8, p.shape[1]), axis=0)
            acc_c = _dot(vt_c[...], p.astype(jnp.bfloat16))
            l = l_c if l is None else l + l_c
            acc = acc_c if acc is None else acc + acc_c
            continue
        s_max = jnp.max(s, axis=0, keepdims=True)
        if m is None:
            m = s_max
            p = jnp.exp2(s - m)
            l = jnp.sum(p, axis=0, keepdims=True)
            acc = _dot(vt_c[...], p.astype(jnp.bfloat16))
        else:
            m_new = jnp.maximum(m, s_max)
            alpha = jnp.exp2(m - m_new)
            p = jnp.exp2(s - m_new)
            l = alpha * l + jnp.sum(p, axis=0, keepdims=True)
            acc = alpha * acc + _dot(vt_c[...], p.astype(jnp.bfloat16))
            m = m_new
    if bounded:
        l = jnp.sum(l, axis=0, keepdims=True)
    o = (acc / l).T
    for g in range(GQA_GROUP):
        o_ref[:, g * HEAD_DIM:(g + 1) * HEAD_DIM] = o[g * tq:(g + 1) * tq].astype(o_ref.dtype)


def _attention_call(shift, qt, kv_sources, *, tq, bounded):
    b, d, n = qt.shape
    gw = GQA_GROUP * HEAD_DIM
    in_specs = [pl.BlockSpec(memory_space=pltpu.SMEM),
                pl.BlockSpec((None, gw, tq), lambda bi, kh, j: (bi, kh, j))]
    args = [shift, qt]
    for k, vt in kv_sources:
        length = k.shape[1]
        in_specs.append(pl.BlockSpec((None, length, HEAD_DIM), lambda bi, kh, j: (bi, 0, kh)))
        in_specs.append(pl.BlockSpec((None, HEAD_DIM, length), lambda bi, kh, j: (bi, kh, 0)))
        args += [k, vt]
    return pl.pallas_call(
        functools.partial(_attn_kernel, n_src=len(kv_sources), bounded=bounded),
        grid=(b, N_KV_HEADS, n // tq),
        in_specs=in_specs,
        out_specs=pl.BlockSpec((None, tq, gw), lambda bi, kh, j: (bi, j, kh)),
        out_shape=jax.ShapeDtypeStruct((b, n, d), jnp.bfloat16),
        compiler_params=_params(3),
        name="attention_bounded" if bounded else "attention_online",
    )(*args)


MAX_SAFE_SCORE_BOUND = 40.0


def _attention(q, kv_sources, score_bound, *, tq):
    return lax.cond(
        score_bound <= MAX_SAFE_SCORE_BOUND,
        functools.partial(_attention_call, tq=tq, bounded=True),
        functools.partial(_attention_call, tq=tq, bounded=False),
        score_bound.reshape(1, 1), q, kv_sources)


def _head_perm(a, axis):
    shape = a.shape
    a = a.reshape(shape[:axis] + (2, 2, ROPE_FREQS) + shape[axis + 1:])
    return jnp.swapaxes(a, axis, axis + 1).reshape(shape)


def _rope_tables(n):
    pos = jnp.arange(n, dtype=jnp.int32)
    row = (pos // GRID_W).astype(jnp.float32)
    col = (pos % GRID_W).astype(jnp.float32)
    inv_freq = ROPE_THETA ** (-jnp.arange(ROPE_FREQS, dtype=jnp.float32) / ROPE_FREQS)
    ang = jnp.concatenate([inv_freq[:, None] * row[None, :], inv_freq[:, None] * col[None, :]],
                          axis=0)
    return jnp.cos(ang), jnp.sin(ang)


def _tiles(n, n_ctx_total):
    tm_lat = 1024 if n % 1024 == 0 else n
    tm_ctx_ffn = 1024 if n_ctx_total % 1024 == 0 else n_ctx_total
    tm_qkv = 1024 if n % 1024 == 0 else n
    tq = 256 if n % 256 == 0 else n
    return tm_lat, tm_ctx_ffn, tm_qkv, tq


def kernel(x, c, ctx, c_ctx, ada_w, ada_b, norm_g, final_g, ffn_w_in, ffn_w_out,
           conv_w_in, conv_w, conv_w_out, attn_w_qkv, attn_q_g, attn_k_g, attn_w_o):
    b, n, d = x.shape
    n_ctx = ctx.shape[1]
    bf = jnp.bfloat16
    ffn_w_in_b, ffn_w_out_b = ffn_w_in.astype(bf), ffn_w_out.astype(bf)
    conv_w_in_b, conv_w_out_b = conv_w_in.astype(bf), conv_w_out.astype(bf)
    attn_w_o_b = attn_w_o.astype(bf)
    n_attn = attn_q_g.shape[0]
    n_qk = D_MODEL + KV_DIM
    w_qk = _head_perm(attn_w_qkv[:, :, :n_qk].reshape(n_attn, d, N_HEADS + N_KV_HEADS, HEAD_DIM), 3)
    w_qkv_perm = jnp.concatenate([w_qk.reshape(n_attn, d, n_qk), attn_w_qkv[:, :, n_qk:]], axis=2)
    attn_wt_qkv_b = w_qkv_perm.transpose(0, 2, 1).astype(bf)
    q_scale = HEAD_DIM ** -0.5 * math.log2(math.e)
    q_g = jnp.broadcast_to((_head_perm(attn_q_g, 1) * q_scale)[:, :, None],
                           (n_attn, HEAD_DIM, LANES))
    k_g = jnp.broadcast_to(_head_perm(attn_k_g, 1)[:, :, None], (n_attn, HEAD_DIM, LANES))
    norm_g_rows = norm_g.reshape(DEPTH * N_SUBLAYERS, 1, d)

    cvec = jnp.zeros((MOD_ROWS, d), jnp.float32).at[:b].set(c).at[b].set(c_ctx)
    mods = _ada_mods(cvec, ada_w, ada_b).reshape(DEPTH, MOD_ROWS, N_SUBLAYERS, N_MOD, d)
    rope_tabs = _rope_tables(n)

    tm_lat, tm_ctx_ffn, tm_qkv, tq = _tiles(n, b * n_ctx)
    tq_ctx = min(tq, n_ctx)
    x_lat, x_ctx = x, ctx

    def ffn_ctx(x_c, mod, layer, which, attn=None):
        if attn is not None:
            attn = (attn[0].reshape(1, b * n_ctx, d),) + attn[1:]
        y = _ffn(x_c.reshape(1, b * n_ctx, d), mod, norm_g_rows, ffn_w_in_b, ffn_w_out_b,
                 layer, which, tm=tm_ctx_ffn, attn=attn)
        return y.reshape(b, n_ctx, d)

    for i in range(DEPTH):
        is_attn = (i % N_MIXERS) == 1
        last = i == DEPTH - 1
        ctx_feeds_mixer = (not last) or is_attn
        j = i // N_MIXERS
        m_lat = [mods[i, :b, s] for s in range(N_SUBLAYERS)]
        m_ctx = [mods[i, b:b + 1, s] for s in range(N_SUBLAYERS)]
        x_lat = _ffn(x_lat, m_lat[0], norm_g_rows, ffn_w_in_b, ffn_w_out_b, i, 0, tm=tm_lat)
        if ctx_feeds_mixer:
            x_ctx = ffn_ctx(x_ctx, m_ctx[0], i, 0)
        attn_lat = attn_ctx = None
        if is_attn:
            q_l, k_l, v_l = _qkv_proj(x_lat, m_lat[1], norm_g_rows, attn_wt_qkv_b, q_g, k_g,
                                      rope_tabs, i, j, tm=tm_qkv)
            q_c, k_c, v_c = _qkv_proj(x_ctx, m_ctx[1], norm_g_rows, attn_wt_qkv_b, q_g, k_g,
                                      None, i, j, tm=n_ctx)
            score_bound = (1.02 * HEAD_DIM * q_scale * jnp.max(jnp.abs(attn_q_g[j]))
                           * jnp.max(jnp.abs(attn_k_g[j])))
            o_l = _attention(q_l, [(k_c, v_c), (k_l, v_l)], score_bound, tq=tq)
            attn_lat = (o_l, m_lat[1], attn_w_o_b, j)
            if not last:
                o_c = _attention(q_c, [(k_c, v_c)], score_bound, tq=tq_ctx)
                attn_ctx = (o_c, m_ctx[1], attn_w_o_b, j)
        else:
            x_lat = _conv_mixer(x_lat, m_lat[1], norm_g_rows, conv_w_in_b, conv_w, conv_w_out_b,
                                i, j, tm=tm_lat)
            if not last:
                x_ctx = _conv_mixer(x_ctx, m_ctx[1], norm_g_rows, conv_w_in_b, conv_w,
                                    conv_w_out_b, i, j, tm=n_ctx)
        x_lat = _ffn(x_lat, m_lat[2], norm_g_rows, ffn_w_in_b, ffn_w_out_b, i, 1, tm=tm_lat,
                     attn=attn_lat, final_g=final_g if last else None)
        if not last:
            x_ctx = ffn_ctx(x_ctx, m_ctx[2], i, 1, attn=attn_ctx)
    return x_lat
```

```python
import functools
import math

import jax
import jax.numpy as jnp
from jax import lax
from jax.experimental import pallas as pl
from jax.experimental.pallas import tpu as pltpu

D_MODEL = 1024
DEPTH = 4
GRID_W = 64
N_MIXERS = 2
HEAD_DIM = 128
N_HEADS = D_MODEL // HEAD_DIM
N_KV_HEADS = N_HEADS // 4
GQA_GROUP = N_HEADS // N_KV_HEADS
KV_DIM = N_KV_HEADS * HEAD_DIM
QKV_DIM = (N_HEADS + 2 * N_KV_HEADS) * HEAD_DIM
ROPE_THETA = 10000.0
ROPE_FREQS = HEAD_DIM // 4
D_FF = 2816
N_SUBLAYERS = 3
N_MOD = 3
NORM_EPS = 1e-6
MACARON_WEIGHT = 0.5

VMEM_LIMIT_BYTES = 56 * 1024 * 1024
MOD_ROWS = 16
ADA_COL_BLOCK = 1024
FFN_CHUNKS = ((0, 1536), (1536, D_FF))
FFN_SUB_ROWS = 512


def _params(n_grid_dims):
    return pltpu.CompilerParams(
        dimension_semantics=("arbitrary",) * n_grid_dims,
        vmem_limit_bytes=VMEM_LIMIT_BYTES)


def _resident(shape, at=()):
    index = tuple(at) + (0,) * len(shape)
    return pl.BlockSpec((None,) * len(at) + tuple(shape), lambda *_: index,
                        pipeline_mode=pl.Buffered(1))


def _sigmoid(x):
    return 1.0 / (1.0 + jnp.exp(-x))


def _norm_modulate(x, g, mod_ref):
    ms = jnp.mean(x * x, axis=-1, keepdims=True)
    a = g * (1.0 + mod_ref[1:2, :])
    h = (x * lax.rsqrt(ms + NORM_EPS)) * a + mod_ref[0:1, :]
    return h.astype(jnp.bfloat16)


def _dot(a, b):
    return jnp.dot(a, b, preferred_element_type=jnp.float32)


def _ada_kernel(c_ref, w_ref, b_ref, o_ref):
    c = c_ref[...]
    s = (c * _sigmoid(c)).astype(jnp.bfloat16)
    o_ref[...] = _dot(s, w_ref[...].astype(jnp.bfloat16)) + b_ref[...]


def _ada_mods(cvec, ada_w, ada_b):
    depth, d, n = ada_w.shape
    return pl.pallas_call(
        _ada_kernel,
        grid=(depth, n // ADA_COL_BLOCK),
        in_specs=[
            pl.BlockSpec((MOD_ROWS, d), lambda i, j: (0, 0)),
            pl.BlockSpec((None, d, ADA_COL_BLOCK), lambda i, j: (i, 0, j)),
            pl.BlockSpec((None, 1, ADA_COL_BLOCK), lambda i, j: (i, 0, j)),
        ],
        out_specs=pl.BlockSpec((None, MOD_ROWS, ADA_COL_BLOCK), lambda i, j: (i, 0, j)),
        out_shape=jax.ShapeDtypeStruct((depth, MOD_ROWS, n), jnp.float32),
        compiler_params=_params(2),
        name="ada_mods",
    )(cvec, ada_w, ada_b.reshape(depth, 1, n))


def _mod_spec(per_group):
    if per_group:
        return pl.BlockSpec((None, N_MOD, D_MODEL), lambda b, j: (b, 0, 0))
    return pl.BlockSpec((None, N_MOD, D_MODEL), lambda b, j: (0, 0, 0))


def _row_spec(tm, width=D_MODEL):
    return pl.BlockSpec((None, tm, width), lambda b, j: (b, j, 0))


def _ffn_kernel(x_ref, mod_ref, g_ref, w_in_ref, w_out_ref, *rest, attn_proj, final_norm):
    rest = list(rest)
    if attn_proj:
        ao_ref, mix_mod_ref, wo_ref = rest[:3]
        rest = rest[3:]
    if final_norm:
        fg_ref = rest.pop(0)
    (o_ref,) = rest
    tm = x_ref.shape[0]
    sub = min(tm, FFN_SUB_ROWS)
    subs = [pl.ds(r, sub) for r in range(0, tm, sub)]
    if attn_proj:
        for rows in subs:
            o_ref[rows, :] = x_ref[rows, :] + mix_mod_ref[2:3, :] * _dot(ao_ref[rows, :], wo_ref[...])
        res_ref = o_ref
    else:
        res_ref = x_ref
    hs = [_norm_modulate(res_ref[rows, :], g_ref[...], mod_ref) for rows in subs]
    for rows, h in zip(subs, hs):
        acc = None
        for lo, hi in FFN_CHUNKS:
            gate = _dot(h, w_in_ref[:, lo:hi])
            up = _dot(h, w_in_ref[:, D_FF + lo:D_FF + hi])
            act = (gate * _sigmoid(gate) * up).astype(jnp.bfloat16)
            part = _dot(act, w_out_ref[lo:hi, :])
            acc = part if acc is None else acc + part
        y = res_ref[rows, :] + (MACARON_WEIGHT * mod_ref[2:3, :]) * acc
        if final_norm:
            ms = jnp.mean(y * y, axis=-1, keepdims=True)
            y = (y * lax.rsqrt(ms + NORM_EPS)) * fg_ref[...]
        o_ref[rows, :] = y


def _ffn(x, mod, norm_g, w_in, w_out, layer, which, *, tm, attn=None, final_g=None):
    groups, t, d = x.shape
    in_specs = [_row_spec(tm), _mod_spec(mod.shape[0] > 1),
                _resident((1, d), at=(layer * N_SUBLAYERS + 2 * which,)),
                _resident(w_in.shape[2:], at=(layer, which)),
                _resident(w_out.shape[2:], at=(layer, which))]
    args = [x, mod, norm_g, w_in, w_out]
    if attn is not None:
        attn_out, mix_mod, w_o, attn_which = attn
        in_specs += [_row_spec(tm), _mod_spec(mix_mod.shape[0] > 1),
                     _resident(w_o.shape[1:], at=(attn_which,))]
        args += [attn_out, mix_mod, w_o]
    if final_g is not None:
        in_specs.append(_resident((1, d)))
        args.append(final_g.reshape(1, d))
    return pl.pallas_call(
        functools.partial(_ffn_kernel, attn_proj=attn is not None, final_norm=final_g is not None),
        grid=(groups, t // tm),
        in_specs=in_specs,
        out_specs=_row_spec(tm),
        out_shape=jax.ShapeDtypeStruct(x.shape, jnp.float32),
        compiler_params=_params(2),
        name="ffn",
    )(*args)


HALO = 8


CONV_CHUNKS = ((0, 512), (512, D_MODEL))


def _conv_kernel(x_ref, xp_ref, xn_ref, mod_ref, g_ref, w_in_ref, cw_ref, w_out_ref, o_ref):
    j = pl.program_id(1)
    nj = pl.num_programs(1)
    d = D_MODEL
    tm = x_ref.shape[0]
    n_ext = tm + 2 * HALO
    main = slice(HALO, HALO + tm)
    x_ext = jnp.concatenate([xp_ref[...], x_ref[...], xn_ref[...]], axis=0)
    h = _norm_modulate(x_ext, g_ref[...], mod_ref)
    row = lax.broadcasted_iota(jnp.int32, (n_ext, 1), 0)
    valid = ((row >= HALO) | (j > 0)) & ((row < HALO + tm) | (j < nj - 1))
    acc = None
    for lo, hi in CONV_CHUNKS:
        b_gate = _dot(h, w_in_ref[:, lo:hi])[main]
        u = _dot(h, w_in_ref[:, d + lo:d + hi]) * _dot(h, w_in_ref[:, 2 * d + lo:2 * d + hi])
        u = jnp.where(valid, u, 0.0)
        y = (cw_ref[0:1, lo:hi] * pltpu.roll(u, 1, 0)[main] + cw_ref[1:2, lo:hi] * u[main]
             + cw_ref[2:3, lo:hi] * pltpu.roll(u, n_ext - 1, 0)[main])
        part = _dot((b_gate * y).astype(jnp.bfloat16), w_out_ref[lo:hi, :])
        acc = part if acc is None else acc + part
    o_ref[...] = x_ref[...] + mod_ref[2:3, :] * acc


def _conv_mixer(x, mod, norm_g, w_in, conv_w, w_out, layer, which, *, tm):
    groups, t, d = x.shape
    nj = t // tm
    hb = tm // HALO
    last_halo_block = t // HALO - 1
    prev_spec = pl.BlockSpec((None, HALO, d), lambda b, j: (b, jnp.maximum(j * hb - 1, 0), 0))
    next_spec = pl.BlockSpec((None, HALO, d),
                             lambda b, j: (b, jnp.minimum((j + 1) * hb, last_halo_block), 0))
    return pl.pallas_call(
        _conv_kernel,
        grid=(groups, nj),
        in_specs=[_row_spec(tm), prev_spec, next_spec, _mod_spec(mod.shape[0] > 1),
                  _resident((1, d), at=(layer * N_SUBLAYERS + 1,)),
                  _resident(w_in.shape[1:], at=(which,)), _resident(conv_w.shape[1:], at=(which,)),
                  _resident(w_out.shape[1:], at=(which,))],
        out_specs=_row_spec(tm),
        out_shape=jax.ShapeDtypeStruct(x.shape, jnp.float32),
        compiler_params=_params(2),
        name="conv_mixer",
    )(x, x, x, mod, norm_g, w_in, conv_w, w_out)


LANES = 128
def _qkv_kernel(x_ref, mod_ref, g_ref, wt_ref, qg_ref, kg_ref, *rest, rope):
    if rope:
        cos_ref, sin_ref, q_ref, k_ref, v_ref = rest
        cos, sin = cos_ref[...], sin_ref[...]
    else:
        q_ref, k_ref, v_ref = rest
    tm = x_ref.shape[0]
    h = _norm_modulate(x_ref[...], g_ref[...], mod_ref)
    yt = lax.dot_general(wt_ref[...], h, (((1,), (1,)), ((), ())),
                         preferred_element_type=jnp.float32)
    qg = jnp.tile(qg_ref[...], (1, tm // LANES))
    kg = jnp.tile(kg_ref[...], (1, tm // LANES))
    half = HEAD_DIM // 2
    for hd in range(N_HEADS + N_KV_HEADS):
        xh = yt[hd * HEAD_DIM:(hd + 1) * HEAD_DIM, :]
        is_q = hd < N_HEADS
        ms = jnp.mean(xh * xh, axis=0, keepdims=True)
        xh = (xh * lax.rsqrt(ms + NORM_EPS)) * (qg if is_q else kg)
        if rope:
            x1, x2 = xh[:half], xh[half:]
            xh = jnp.concatenate([x1 * cos - x2 * sin, x2 * cos + x1 * sin], axis=0)
        if is_q:
            q_ref[hd * HEAD_DIM:(hd + 1) * HEAD_DIM, :] = xh.astype(q_ref.dtype)
        else:
            kd = hd - N_HEADS
            k_ref[:, kd * HEAD_DIM:(kd + 1) * HEAD_DIM] = xh.T.astype(k_ref.dtype)
    v_ref[...] = yt[D_MODEL + KV_DIM:, :].astype(v_ref.dtype)


def _qkv_proj(x, mod, norm_g, wt_qkv, q_g, k_g, rope_tabs, layer, which, *, tm):
    groups, t, d = x.shape
    rope = rope_tabs is not None
    in_specs = [_row_spec(tm), _mod_spec(mod.shape[0] > 1),
                _resident((1, d), at=(layer * N_SUBLAYERS + 1,)),
                _resident(wt_qkv.shape[1:], at=(which,)),
                _resident((HEAD_DIM, LANES), at=(which,)), _resident((HEAD_DIM, LANES), at=(which,))]
    args = [x, mod, norm_g, wt_qkv, q_g, k_g]
    if rope:
        in_specs += [pl.BlockSpec((HEAD_DIM // 2, tm), lambda b, j: (0, j))] * 2
        args += list(rope_tabs)
    return pl.pallas_call(
        functools.partial(_qkv_kernel, rope=rope),
        grid=(groups, t // tm),
        in_specs=in_specs,
        out_specs=[pl.BlockSpec((None, d, tm), lambda b, j: (b, 0, j)), _row_spec(tm, KV_DIM),
                   pl.BlockSpec((None, KV_DIM, tm), lambda b, j: (b, 0, j))],
        out_shape=[jax.ShapeDtypeStruct((groups, d, t), jnp.bfloat16),
                   jax.ShapeDtypeStruct((groups, t, KV_DIM), jnp.bfloat16),
                   jax.ShapeDtypeStruct((groups, KV_DIM, t), jnp.bfloat16)],
        compiler_params=_params(2),
        name="qkv_proj",
    )(*args)


ATTN_KEY_CHUNK = 2048
ATTN_Q_PASS = 256


def _attn_kernel(shift_ref, q_ref, *rest, n_src, bounded):
    kv_refs, o_ref = rest[:2 * n_src], rest[2 * n_src]
    chunks = []
    for src in range(n_src):
        k_ref, vt_ref = kv_refs[2 * src], kv_refs[2 * src + 1]
        length = k_ref.shape[0]
        tk = min(ATTN_KEY_CHUNK, length)
        chunks += [(k_ref.at[lo:lo + tk, :], vt_ref.at[:, lo:lo + tk]) for lo in range(0, length, tk)]
    tq = min(q_ref.shape[1], ATTN_Q_PASS)
    for q0 in range(0, q_ref.shape[1], tq):
        _attn_pass(shift_ref, q_ref.at[:, q0:q0 + tq], chunks, o_ref.at[q0:q0 + tq, :], bounded)


def _attn_pass(shift_ref, q_ref, chunks, o_ref, bounded):
    tq = q_ref.shape[1]
    qt = jnp.concatenate([q_ref[g * HEAD_DIM:(g + 1) * HEAD_DIM, :] for g in range(GQA_GROUP)],
                         axis=1)

    def scores(c):
        return _dot(chunks[c][0][...], qt)

    m = l = acc = None
    s_next = scores(0)
    for c, (_, vt_c) in enumerate(chunks):
        s = s_next
        if c + 1 < len(chunks):
            s_next = scores(c + 1)
        if bounded:
            p = jnp.exp2(s - shift_ref[0, 0])
            l_c = jnp.sum(p.reshape(p.shape[0] // 8, 8, p.shape[1]), axis=0)
            acc_c = _dot(vt_c[...], p.astype(jnp.bfloat16))
            l = l_c if l is None else l + l_c
            acc = acc_c if acc is None else acc + acc_c
            continue
        s_max = jnp.max(s, axis=0, keepdims=True)
        if m is None:
            m = s_max
            p = jnp.exp2(s - m)
            l = jnp.sum(p, axis=0, keepdims=True)
            acc = _dot(vt_c[...], p.astype(jnp.bfloat16))
        else:
            m_new = jnp.maximum(m, s_max)
            alpha = jnp.exp2(m - m_new)
            p = jnp.exp2(s - m_new)
            l = alpha * l + jnp.sum(p, axis=0, keepdims=True)
            acc = alpha * acc + _dot(vt_c[...], p.astype(jnp.bfloat16))
            m = m_new
    if bounded:
        l = jnp.sum(l, axis=0, keepdims=True)
    o = (acc / l).T
    for g in range(GQA_GROUP):
        o_ref[:, g * HEAD_DIM:(g + 1) * HEAD_DIM] = o[g * tq:(g + 1) * tq].astype(o_ref.dtype)


def _attention_call(shift, qt, kv_sources, *, tq, bounded):
    b, d, n = qt.shape
    gw = GQA_GROUP * HEAD_DIM
    in_specs = [pl.BlockSpec(memory_space=pltpu.SMEM),
                pl.BlockSpec((None, gw, tq), lambda bi, kh, j: (bi, kh, j))]
    args = [shift, qt]
    for k, vt in kv_sources:
        length = k.shape[1]
        in_specs.append(pl.BlockSpec((None, length, HEAD_DIM), lambda bi, kh, j: (bi, 0, kh)))
        in_specs.append(pl.BlockSpec((None, HEAD_DIM, length), lambda bi, kh, j: (bi, kh, 0)))
        args += [k, vt]
    return pl.pallas_call(
        functools.partial(_attn_kernel, n_src=len(kv_sources), bounded=bounded),
        grid=(b, N_KV_HEADS, n // tq),
        in_specs=in_specs,
        out_specs=pl.BlockSpec((None, tq, gw), lambda bi, kh, j: (bi, j, kh)),
        out_shape=jax.ShapeDtypeStruct((b, n, d), jnp.bfloat16),
        compiler_params=_params(3),
        name="attention_bounded" if bounded else "attention_online",
    )(*args)


MAX_SAFE_SCORE_BOUND = 40.0


def _attention(q, kv_sources, score_bound, *, tq):
    return lax.cond(
        score_bound <= MAX_SAFE_SCORE_BOUND,
        functools.partial(_attention_call, tq=tq, bounded=True),
        functools.partial(_attention_call, tq=tq, bounded=False),
        score_bound.reshape(1, 1), q, kv_sources)


def _head_perm(a, axis):
    shape = a.shape
    a = a.reshape(shape[:axis] + (2, 2, ROPE_FREQS) + shape[axis + 1:])
    return jnp.swapaxes(a, axis, axis + 1).reshape(shape)


def _rope_tables(n):
    pos = jnp.arange(n, dtype=jnp.int32)
    row = (pos // GRID_W).astype(jnp.float32)
    col = (pos % GRID_W).astype(jnp.float32)
    inv_freq = ROPE_THETA ** (-jnp.arange(ROPE_FREQS, dtype=jnp.float32) / ROPE_FREQS)
    ang = jnp.concatenate([inv_freq[:, None] * row[None, :], inv_freq[:, None] * col[None, :]],
                          axis=0)
    return jnp.cos(ang), jnp.sin(ang)


def _tiles(n, n_ctx_total):
    tm_lat = 1024 if n % 1024 == 0 else n
    tm_ctx_ffn = 1024 if n_ctx_total % 1024 == 0 else n_ctx_total
    tm_qkv = 1024 if n % 1024 == 0 else n
    tq = 1024 if n % 1024 == 0 else n
    return tm_lat, tm_ctx_ffn, tm_qkv, tq


def kernel(x, c, ctx, c_ctx, ada_w, ada_b, norm_g, final_g, ffn_w_in, ffn_w_out,
           conv_w_in, conv_w, conv_w_out, attn_w_qkv, attn_q_g, attn_k_g, attn_w_o):
    b, n, d = x.shape
    n_ctx = ctx.shape[1]
    bf = jnp.bfloat16
    ffn_w_in_b, ffn_w_out_b = ffn_w_in.astype(bf), ffn_w_out.astype(bf)
    conv_w_in_b, conv_w_out_b = conv_w_in.astype(bf), conv_w_out.astype(bf)
    attn_w_o_b = attn_w_o.astype(bf)
    n_attn = attn_q_g.shape[0]
    n_qk = D_MODEL + KV_DIM
    w_qk = _head_perm(attn_w_qkv[:, :, :n_qk].reshape(n_attn, d, N_HEADS + N_KV_HEADS, HEAD_DIM), 3)
    w_qkv_perm = jnp.concatenate([w_qk.reshape(n_attn, d, n_qk), attn_w_qkv[:, :, n_qk:]], axis=2)
    attn_wt_qkv_b = w_qkv_perm.transpose(0, 2, 1).astype(bf)
    q_scale = HEAD_DIM ** -0.5 * math.log2(math.e)
    q_g = jnp.broadcast_to((_head_perm(attn_q_g, 1) * q_scale)[:, :, None],
                           (n_attn, HEAD_DIM, LANES))
    k_g = jnp.broadcast_to(_head_perm(attn_k_g, 1)[:, :, None], (n_attn, HEAD_DIM, LANES))
    norm_g_rows = norm_g.reshape(DEPTH * N_SUBLAYERS, 1, d)

    cvec = jnp.zeros((MOD_ROWS, d), jnp.float32).at[:b].set(c).at[b].set(c_ctx)
    mods = _ada_mods(cvec, ada_w, ada_b).reshape(DEPTH, MOD_ROWS, N_SUBLAYERS, N_MOD, d)
    rope_tabs = _rope_tables(n)

    tm_lat, tm_ctx_ffn, tm_qkv, tq = _tiles(n, b * n_ctx)
    tq_ctx = min(tq, n_ctx)
    x_lat, x_ctx = x, ctx

    def ffn_ctx(x_c, mod, layer, which, attn=None):
        if attn is not None:
            attn = (attn[0].reshape(1, b * n_ctx, d),) + attn[1:]
        y = _ffn(x_c.reshape(1, b * n_ctx, d), mod, norm_g_rows, ffn_w_in_b, ffn_w_out_b,
                 layer, which, tm=tm_ctx_ffn, attn=attn)
        return y.reshape(b, n_ctx, d)

    for i in range(DEPTH):
        is_attn = (i % N_MIXERS) == 1
        last = i == DEPTH - 1
        ctx_feeds_mixer = (not last) or is_attn
        j = i // N_MIXERS
        m_lat = [mods[i, :b, s] for s in range(N_SUBLAYERS)]
        m_ctx = [mods[i, b:b + 1, s] for s in range(N_SUBLAYERS)]
        x_lat = _ffn(x_lat, m_lat[0], norm_g_rows, ffn_w_in_b, ffn_w_out_b, i, 0, tm=tm_lat)
        if ctx_feeds_mixer:
            x_ctx = ffn_ctx(x_ctx, m_ctx[0], i, 0)
        attn_lat = attn_ctx = None
        if is_attn:
            q_l, k_l, v_l = _qkv_proj(x_lat, m_lat[1], norm_g_rows, attn_wt_qkv_b, q_g, k_g,
                                      rope_tabs, i, j, tm=tm_qkv)
            q_c, k_c, v_c = _qkv_proj(x_ctx, m_ctx[1], norm_g_rows, attn_wt_qkv_b, q_g, k_g,
                                      None, i, j, tm=n_ctx)
            score_bound = (1.02 * HEAD_DIM * q_scale * jnp.max(jnp.abs(attn_q_g[j]))
                           * jnp.max(jnp.abs(attn_k_g[j])))
            o_l = _attention(q_l, [(k_c, v_c), (k_l, v_l)], score_bound, tq=tq)
            attn_lat = (o_l, m_lat[1], attn_w_o_b, j)
            if not last:
                o_c = _attention(q_c, [(k_c, v_c)], score_bound, tq=tq_ctx)
                attn_ctx = (o_c, m_ctx[1], attn_w_o_b, j)
        else:
            x_lat = _conv_mixer(x_lat, m_lat[1], norm_g_rows, conv_w_in_b, conv_w, conv_w_out_b,
                                i, j, tm=tm_lat)
            if not last:
                x_ctx = _conv_mixer(x_ctx, m_ctx[1], norm_g_rows, conv_w_in_b, conv_w,
                                    conv_w_out_b, i, j, tm=n_ctx)
        x_lat = _ffn(x_lat, m_lat[2], norm_g_rows, ffn_w_in_b, ffn_w_out_b, i, 1, tm=tm_lat,
                     attn=attn_lat, final_g=final_g if last else None)
        if not last:
            x_ctx = ffn_ctx(x_ctx, m_ctx[2], i, 1, attn=attn_ctx)
    return x_lat
```

```python
import functools
import math

import jax
import jax.numpy as jnp
from jax import lax
from jax.experimental import pallas as pl
from jax.experimental.pallas import tpu as pltpu

D_MODEL = 1024
DEPTH = 4
GRID_W = 64
N_MIXERS = 2
HEAD_DIM = 128
N_HEADS = D_MODEL // HEAD_DIM
N_KV_HEADS = N_HEADS // 4
GQA_GROUP = N_HEADS // N_KV_HEADS
KV_DIM = N_KV_HEADS * HEAD_DIM
QKV_DIM = (N_HEADS + 2 * N_KV_HEADS) * HEAD_DIM
ROPE_THETA = 10000.0
ROPE_FREQS = HEAD_DIM // 4
D_FF = 2816
N_SUBLAYERS = 3
N_MOD = 3
NORM_EPS = 1e-6
MACARON_WEIGHT = 0.5

VMEM_LIMIT_BYTES = 56 * 1024 * 1024
MOD_ROWS = 16
ADA_COL_BLOCK = 1024
FFN_CHUNKS = ((0, 1536), (1536, D_FF))
FFN_SUB_ROWS = 512


def _params(n_grid_dims):
    return pltpu.CompilerParams(
        dimension_semantics=("arbitrary",) * n_grid_dims,
        vmem_limit_bytes=VMEM_LIMIT_BYTES)


def _resident(shape, at=()):
    index = tuple(at) + (0,) * len(shape)
    return pl.BlockSpec((None,) * len(at) + tuple(shape), lambda *_: index,
                        pipeline_mode=pl.Buffered(1))


def _sigmoid(x):
    return 1.0 / (1.0 + jnp.exp(-x))


def _norm_modulate(x, g, mod_ref):
    ms = jnp.mean(x * x, axis=-1, keepdims=True)
    a = g * (1.0 + mod_ref[1:2, :])
    h = (x * lax.rsqrt(ms + NORM_EPS)) * a + mod_ref[0:1, :]
    return h.astype(jnp.bfloat16)


def _dot(a, b):
    return jnp.dot(a, b, preferred_element_type=jnp.float32)


def _ada_kernel(c_ref, w_ref, b_ref, o_ref):
    c = c_ref[...]
    s = (c * _sigmoid(c)).astype(jnp.bfloat16)
    o_ref[...] = _dot(s, w_ref[...].astype(jnp.bfloat16)) + b_ref[...]


def _ada_mods(cvec, ada_w, ada_b):
    depth, d, n = ada_w.shape
    return pl.pallas_call(
        _ada_kernel,
        grid=(depth, n // ADA_COL_BLOCK),
        in_specs=[
            pl.BlockSpec((MOD_ROWS, d), lambda i, j: (0, 0)),
            pl.BlockSpec((None, d, ADA_COL_BLOCK), lambda i, j: (i, 0, j)),
            pl.BlockSpec((None, 1, ADA_COL_BLOCK), lambda i, j: (i, 0, j)),
        ],
        out_specs=pl.BlockSpec((None, MOD_ROWS, ADA_COL_BLOCK), lambda i, j: (i, 0, j)),
        out_shape=jax.ShapeDtypeStruct((depth, MOD_ROWS, n), jnp.float32),
        compiler_params=_params(2),
        name="ada_mods",
    )(cvec, ada_w, ada_b.reshape(depth, 1, n))


def _mod_spec(per_group):
    if per_group:
        return pl.BlockSpec((None, N_MOD, D_MODEL), lambda b, j: (b, 0, 0))
    return pl.BlockSpec((None, N_MOD, D_MODEL), lambda b, j: (0, 0, 0))


def _row_spec(tm, width=D_MODEL):
    return pl.BlockSpec((None, tm, width), lambda b, j: (b, j, 0))


def _ffn_kernel(*refs, n_x, n_ao, n_lat_tiles, final_norm):
    refs = list(refs)
    x_refs, refs = refs[:n_x], refs[n_x:]
    mod_ref, g_ref, w_in_ref, w_out_ref = refs[:4]
    refs = refs[4:]
    if n_ao:
        ao_refs, (mix_mod_ref, wo_ref), refs = refs[:n_ao], refs[n_ao:n_ao + 2], refs[n_ao + 2:]
    if final_norm:
        fg_ref = refs.pop(0)
    (o_ref,) = refs
    is_lat = pl.program_id(0) < n_lat_tiles

    def pick(srcs, rows):
        if len(srcs) == 1:
            return srcs[0][rows, :]
        return jnp.where(is_lat, srcs[0][rows, :], srcs[1][rows, :])

    tm = o_ref.shape[0]
    sub = min(tm, FFN_SUB_ROWS)
    subs = [pl.ds(r, sub) for r in range(0, tm, sub)]
    if n_ao or n_x > 1:
        for rows in subs:
            xs = pick(x_refs, rows)
            if n_ao:
                xs = xs + mix_mod_ref[2:3, :] * _dot(pick(ao_refs, rows), wo_ref[...])
            o_ref[rows, :] = xs
        res_ref = o_ref
    else:
        res_ref = x_refs[0]
    hs = [_norm_modulate(res_ref[rows, :], g_ref[...], mod_ref) for rows in subs]
    for rows, h in zip(subs, hs):
        acc = None
        for lo, hi in FFN_CHUNKS:
            gate = _dot(h, w_in_ref[:, lo:hi])
            up = _dot(h, w_in_ref[:, D_FF + lo:D_FF + hi])
            act = (gate * _sigmoid(gate) * up).astype(jnp.bfloat16)
            part = _dot(act, w_out_ref[lo:hi, :])
            acc = part if acc is None else acc + part
        y = res_ref[rows, :] + (MACARON_WEIGHT * mod_ref[2:3, :]) * acc
        if final_norm:
            ms = jnp.mean(y * y, axis=-1, keepdims=True)
            y = (y * lax.rsqrt(ms + NORM_EPS)) * fg_ref[...]
        o_ref[rows, :] = y


def _tile_specs(n_src, tm, width, n_lat_tiles):
    if n_src == 1:
        return [pl.BlockSpec((tm, width), lambda t: (t, 0))]
    return [pl.BlockSpec((tm, width), lambda t: (jnp.minimum(t, n_lat_tiles - 1), 0)),
            pl.BlockSpec((tm, width), lambda t: (jnp.maximum(t - n_lat_tiles, 0), 0))]


def _ffn(xs, mod, norm_g, w_in, w_out, layer, which, *, tm, n_tiles, n_lat_tiles, tiles_per_group,
         attn=None, final_g=None):
    d = xs[0].shape[1]
    mod_spec = pl.BlockSpec((None, N_MOD, d), lambda t: (t // tiles_per_group, 0, 0))
    in_specs = _tile_specs(len(xs), tm, d, n_lat_tiles) + [
        mod_spec, _resident((1, d), at=(layer * N_SUBLAYERS + 2 * which,)),
        _resident(w_in.shape[2:], at=(layer, which)), _resident(w_out.shape[2:], at=(layer, which))]
    args = list(xs) + [mod, norm_g, w_in, w_out]
    n_ao = 0
    if attn is not None:
        attn_outs, mix_mod, w_o, attn_which = attn
        n_ao = len(attn_outs)
        in_specs += _tile_specs(n_ao, tm, d, n_lat_tiles) + [
            mod_spec, _resident(w_o.shape[1:], at=(attn_which,))]
        args += list(attn_outs) + [mix_mod, w_o]
    if final_g is not None:
        in_specs.append(_resident((1, d)))
        args.append(final_g.reshape(1, d))
    return pl.pallas_call(
        functools.partial(_ffn_kernel, n_x=len(xs), n_ao=n_ao, n_lat_tiles=n_lat_tiles,
                          final_norm=final_g is not None),
        grid=(n_tiles,),
        in_specs=in_specs,
        out_specs=pl.BlockSpec((tm, d), lambda t: (t, 0)),
        out_shape=jax.ShapeDtypeStruct((n_tiles * tm, d), jnp.float32),
        compiler_params=_params(1),
        name="ffn",
    )(*args)


HALO = 8


CONV_CHUNKS = ((0, 512), (512, D_MODEL))


def _conv_kernel(x_ref, xp_ref, xn_ref, mod_ref, g_ref, w_in_ref, cw_ref, w_out_ref, o_ref):
    j = pl.program_id(1)
    nj = pl.num_programs(1)
    d = D_MODEL
    tm = x_ref.shape[0]
    n_ext = tm + 2 * HALO
    main = slice(HALO, HALO + tm)
    x_ext = jnp.concatenate([xp_ref[...], x_ref[...], xn_ref[...]], axis=0)
    h = _norm_modulate(x_ext, g_ref[...], mod_ref)
    row = lax.broadcasted_iota(jnp.int32, (n_ext, 1), 0)
    valid = ((row >= HALO) | (j > 0)) & ((row < HALO + tm) | (j < nj - 1))
    acc = None
    for lo, hi in CONV_CHUNKS:
        b_gate = _dot(h, w_in_ref[:, lo:hi])[main]
        u = _dot(h, w_in_ref[:, d + lo:d + hi]) * _dot(h, w_in_ref[:, 2 * d + lo:2 * d + hi])
        u = jnp.where(valid, u, 0.0)
        y = (cw_ref[0:1, lo:hi] * pltpu.roll(u, 1, 0)[main] + cw_ref[1:2, lo:hi] * u[main]
             + cw_ref[2:3, lo:hi] * pltpu.roll(u, n_ext - 1, 0)[main])
        part = _dot((b_gate * y).astype(jnp.bfloat16), w_out_ref[lo:hi, :])
        acc = part if acc is None else acc + part
    o_ref[...] = x_ref[...] + mod_ref[2:3, :] * acc


def _segment_spec(seg, block_rows, width, index_in_seq):
    row0, _, t = seg
    per_seq = t // block_rows
    base = row0 // block_rows
    return pl.BlockSpec((block_rows, width), lambda b, j: (base + b * per_seq + index_in_seq(j), 0))


def _conv_mixer(s, seg, mod, norm_g, w_in, conv_w, w_out, layer, which, *, tm):
    _, groups, t = seg
    d = s.shape[1]
    nj = t // tm
    hb = tm // HALO
    last_halo_block = t // HALO - 1
    prev_spec = _segment_spec(seg, HALO, d, lambda j: jnp.maximum(j * hb - 1, 0))
    next_spec = _segment_spec(seg, HALO, d, lambda j: jnp.minimum((j + 1) * hb, last_halo_block))
    return pl.pallas_call(
        _conv_kernel,
        grid=(groups, nj),
        in_specs=[_segment_spec(seg, tm, d, lambda j: j), prev_spec, next_spec,
                  _mod_spec(mod.shape[0] > 1),
                  _resident((1, d), at=(layer * N_SUBLAYERS + 1,)),
                  _resident(w_in.shape[1:], at=(which,)), _resident(conv_w.shape[1:], at=(which,)),
                  _resident(w_out.shape[1:], at=(which,))],
        out_specs=_row_spec(tm),
        out_shape=jax.ShapeDtypeStruct((groups, t, d), jnp.float32),
        compiler_params=_params(2),
        name="conv_mixer",
    )(s, s, s, mod, norm_g, w_in, conv_w, w_out)


LANES = 128
def _qkv_kernel(x_ref, mod_ref, g_ref, wt_ref, qg_ref, kg_ref, *rest, rope):
    if rope:
        cos_ref, sin_ref, q_ref, k_ref, v_ref = rest
        cos, sin = cos_ref[...], sin_ref[...]
    else:
        q_ref, k_ref, v_ref = rest
    tm = x_ref.shape[0]
    h = _norm_modulate(x_ref[...], g_ref[...], mod_ref)
    yt = lax.dot_general(wt_ref[...], h, (((1,), (1,)), ((), ())),
                         preferred_element_type=jnp.float32)
    qg = jnp.tile(qg_ref[...], (1, tm // LANES))
    kg = jnp.tile(kg_ref[...], (1, tm // LANES))
    half = HEAD_DIM // 2
    for hd in range(N_HEADS + N_KV_HEADS):
        xh = yt[hd * HEAD_DIM:(hd + 1) * HEAD_DIM, :]
        is_q = hd < N_HEADS
        ms = jnp.mean(xh * xh, axis=0, keepdims=True)
        xh = (xh * lax.rsqrt(ms + NORM_EPS)) * (qg if is_q else kg)
        if rope:
            x1, x2 = xh[:half], xh[half:]
            xh = jnp.concatenate([x1 * cos - x2 * sin, x2 * cos + x1 * sin], axis=0)
        if is_q:
            q_ref[hd * HEAD_DIM:(hd + 1) * HEAD_DIM, :] = xh.astype(q_ref.dtype)
        else:
            kd = hd - N_HEADS
            k_ref[:, kd * HEAD_DIM:(kd + 1) * HEAD_DIM] = xh.T.astype(k_ref.dtype)
    v_ref[...] = yt[D_MODEL + KV_DIM:, :].astype(v_ref.dtype)


def _qkv_proj(s, seg, mod, norm_g, wt_qkv, q_g, k_g, rope_tabs, layer, which, *, tm):
    _, groups, t = seg
    d = s.shape[1]
    x = s
    rope = rope_tabs is not None
    in_specs = [_segment_spec(seg, tm, d, lambda j: j), _mod_spec(mod.shape[0] > 1),
                _resident((1, d), at=(layer * N_SUBLAYERS + 1,)),
                _resident(wt_qkv.shape[1:], at=(which,)),
                _resident((HEAD_DIM, LANES), at=(which,)), _resident((HEAD_DIM, LANES), at=(which,))]
    args = [x, mod, norm_g, wt_qkv, q_g, k_g]
    if rope:
        in_specs += [pl.BlockSpec((HEAD_DIM // 2, tm), lambda b, j: (0, j))] * 2
        args += list(rope_tabs)
    return pl.pallas_call(
        functools.partial(_qkv_kernel, rope=rope),
        grid=(groups, t // tm),
        in_specs=in_specs,
        out_specs=[pl.BlockSpec((None, d, tm), lambda b, j: (b, 0, j)), _row_spec(tm, KV_DIM),
                   pl.BlockSpec((None, KV_DIM, tm), lambda b, j: (b, 0, j))],
        out_shape=[jax.ShapeDtypeStruct((groups, d, t), jnp.bfloat16),
                   jax.ShapeDtypeStruct((groups, t, KV_DIM), jnp.bfloat16),
                   jax.ShapeDtypeStruct((groups, KV_DIM, t), jnp.bfloat16)],
        compiler_params=_params(2),
        name="qkv_proj",
    )(*args)


ATTN_KEY_CHUNK = 2048
ATTN_Q_PASS = 256


def _attn_kernel(shift_ref, q_ref, *rest, n_src, bounded):
    kv_refs, o_ref = rest[:2 * n_src], rest[2 * n_src]
    chunks = []
    for src in range(n_src):
        k_ref, vt_ref = kv_refs[2 * src], kv_refs[2 * src + 1]
        length = k_ref.shape[0]
        tk = min(ATTN_KEY_CHUNK, length)
        chunks += [(k_ref.at[lo:lo + tk, :], vt_ref.at[:, lo:lo + tk]) for lo in range(0, length, tk)]
    tq = min(q_ref.shape[1], ATTN_Q_PASS)
    for q0 in range(0, q_ref.shape[1], tq):
        _attn_pass(shift_ref, q_ref.at[:, q0:q0 + tq], chunks, o_ref.at[q0:q0 + tq, :], bounded)


def _attn_pass(shift_ref, q_ref, chunks, o_ref, bounded):
    tq = q_ref.shape[1]
    qt = jnp.concatenate([q_ref[g * HEAD_DIM:(g + 1) * HEAD_DIM, :] for g in range(GQA_GROUP)],
                         axis=1)

    def scores(c):
        return _dot(chunks[c][0][...], qt)

    m = l = acc = None
    s_next = scores(0)
    for c, (_, vt_c) in enumerate(chunks):
        s = s_next
        if c + 1 < len(chunks):
            s_next = scores(c + 1)
        if bounded:
            p = jnp.exp2(s - shift_ref[0, 0])
            l_c = jnp.sum(p.reshape(p.shape[0] // 8, 8, p.shape[1]), axis=0)
            acc_c = _dot(vt_c[...], p.astype(jnp.bfloat16))
            l = l_c if l is None else l + l_c
            acc = acc_c if acc is None else acc + acc_c
            continue
        s_max = jnp.max(s, axis=0, keepdims=True)
        if m is None:
            m = s_max
            p = jnp.exp2(s - m)
            l = jnp.sum(p, axis=0, keepdims=True)
            acc = _dot(vt_c[...], p.astype(jnp.bfloat16))
        else:
            m_new = jnp.maximum(m, s_max)
            alpha = jnp.exp2(m - m_new)
            p = jnp.exp2(s - m_new)
            l = alpha * l + jnp.sum(p, axis=0, keepdims=True)
            acc = alpha * acc + _dot(vt_c[...], p.astype(jnp.bfloat16))
            m = m_new
    if bounded:
        l = jnp.sum(l, axis=0, keepdims=True)
    o = (acc / l).T
    for g in range(GQA_GROUP):
        o_ref[:, g * HEAD_DIM:(g + 1) * HEAD_DIM] = o[g * tq:(g + 1) * tq].astype(o_ref.dtype)


def _attention_call(shift, qt, kv_sources, *, tq, bounded):
    b, d, n = qt.shape
    gw = GQA_GROUP * HEAD_DIM
    in_specs = [pl.BlockSpec(memory_space=pltpu.SMEM),
                pl.BlockSpec((None, gw, tq), lambda bi, kh, j: (bi, kh, j))]
    args = [shift, qt]
    for k, vt in kv_sources:
        length = k.shape[1]
        in_specs.append(pl.BlockSpec((None, length, HEAD_DIM), lambda bi, kh, j: (bi, 0, kh)))
        in_specs.append(pl.BlockSpec((None, HEAD_DIM, length), lambda bi, kh, j: (bi, kh, 0)))
        args += [k, vt]
    return pl.pallas_call(
        functools.partial(_attn_kernel, n_src=len(kv_sources), bounded=bounded),
        grid=(b, N_KV_HEADS, n // tq),
        in_specs=in_specs,
        out_specs=pl.BlockSpec((None, tq, gw), lambda bi, kh, j: (bi, j, kh)),
        out_shape=jax.ShapeDtypeStruct((b, n, d), jnp.bfloat16),
        compiler_params=_params(3),
        name="attention_bounded" if bounded else "attention_online",
    )(*args)


MAX_SAFE_SCORE_BOUND = 40.0


def _attention(q, kv_sources, score_bound, *, tq):
    return lax.cond(
        score_bound <= MAX_SAFE_SCORE_BOUND,
        functools.partial(_attention_call, tq=tq, bounded=True),
        functools.partial(_attention_call, tq=tq, bounded=False),
        score_bound.reshape(1, 1), q, kv_sources)


def _head_perm(a, axis):
    shape = a.shape
    a = a.reshape(shape[:axis] + (2, 2, ROPE_FREQS) + shape[axis + 1:])
    return jnp.swapaxes(a, axis, axis + 1).reshape(shape)


def _rope_tables(n):
    pos = jnp.arange(n, dtype=jnp.int32)
    row = (pos // GRID_W).astype(jnp.float32)
    col = (pos % GRID_W).astype(jnp.float32)
    inv_freq = ROPE_THETA ** (-jnp.arange(ROPE_FREQS, dtype=jnp.float32) / ROPE_FREQS)
    ang = jnp.concatenate([inv_freq[:, None] * row[None, :], inv_freq[:, None] * col[None, :]],
                          axis=0)
    return jnp.cos(ang), jnp.sin(ang)


def _tiles(n, n_ctx_total):
    tm = 1024
    assert n % tm == 0 and n_ctx_total % tm == 0, (n, n_ctx_total)
    return tm, tm


def kernel(x, c, ctx, c_ctx, ada_w, ada_b, norm_g, final_g, ffn_w_in, ffn_w_out,
           conv_w_in, conv_w, conv_w_out, attn_w_qkv, attn_q_g, attn_k_g, attn_w_o):
    b, n, d = x.shape
    n_ctx = ctx.shape[1]
    bf = jnp.bfloat16
    ffn_w_in_b, ffn_w_out_b = ffn_w_in.astype(bf), ffn_w_out.astype(bf)
    conv_w_in_b, conv_w_out_b = conv_w_in.astype(bf), conv_w_out.astype(bf)
    attn_w_o_b = attn_w_o.astype(bf)
    n_attn = attn_q_g.shape[0]
    n_qk = D_MODEL + KV_DIM
    w_qk = _head_perm(attn_w_qkv[:, :, :n_qk].reshape(n_attn, d, N_HEADS + N_KV_HEADS, HEAD_DIM), 3)
    w_qkv_perm = jnp.concatenate([w_qk.reshape(n_attn, d, n_qk), attn_w_qkv[:, :, n_qk:]], axis=2)
    attn_wt_qkv_b = w_qkv_perm.transpose(0, 2, 1).astype(bf)
    q_scale = HEAD_DIM ** -0.5 * math.log2(math.e)
    q_g = jnp.broadcast_to((_head_perm(attn_q_g, 1) * q_scale)[:, :, None],
                           (n_attn, HEAD_DIM, LANES))
    k_g = jnp.broadcast_to(_head_perm(attn_k_g, 1)[:, :, None], (n_attn, HEAD_DIM, LANES))
    norm_g_rows = norm_g.reshape(DEPTH * N_SUBLAYERS, 1, d)

    cvec = jnp.zeros((MOD_ROWS, d), jnp.float32).at[:b].set(c).at[b].set(c_ctx)
    mods = _ada_mods(cvec, ada_w, ada_b).reshape(DEPTH, MOD_ROWS, N_SUBLAYERS, N_MOD, d)
    rope_tabs = _rope_tables(n)

    tm, tq = _tiles(n, b * n_ctx)
    tq_ctx = min(tq, n_ctx)
    n_lat_tiles = b * n // tm
    n_all_tiles = n_lat_tiles + b * n_ctx // tm
    lat_seg = (0, b, n)
    ctx_seg = (b * n, b, n_ctx)

    def ffn(xs, mod, layer, which, with_ctx, attn=None, final_g=None):
        return _ffn(xs, mod, norm_g_rows, ffn_w_in_b, ffn_w_out_b, layer, which, tm=tm,
                    n_tiles=n_all_tiles if with_ctx else n_lat_tiles, n_lat_tiles=n_lat_tiles,
                    tiles_per_group=n // tm, attn=attn, final_g=final_g)

    stream = [x.reshape(b * n, d), ctx.reshape(b * n_ctx, d)]
    for i in range(DEPTH):
        is_attn = (i % N_MIXERS) == 1
        last = i == DEPTH - 1
        ctx_feeds_mixer = (not last) or is_attn
        j = i // N_MIXERS
        m_all = [mods[i, :b + 1, s] for s in range(N_SUBLAYERS)]
        m_lat = [m[:b] for m in m_all]
        m_ctx = [m[b:] for m in m_all]
        s0 = ffn(stream if ctx_feeds_mixer else stream[:1], m_all[0], i, 0, ctx_feeds_mixer)
        if is_attn:
            q_l, k_l, v_l = _qkv_proj(s0, lat_seg, m_lat[1], norm_g_rows, attn_wt_qkv_b, q_g, k_g,
                                      rope_tabs, i, j, tm=tm)
            q_c, k_c, v_c = _qkv_proj(s0, ctx_seg, m_ctx[1], norm_g_rows, attn_wt_qkv_b, q_g, k_g,
                                      None, i, j, tm=n_ctx)
            score_bound = (1.02 * HEAD_DIM * q_scale * jnp.max(jnp.abs(attn_q_g[j]))
                           * jnp.max(jnp.abs(attn_k_g[j])))
            attn_outs = [_attention(q_l, [(k_c, v_c), (k_l, v_l)], score_bound,
                                    tq=tq).reshape(b * n, d)]
            if not last:
                attn_outs.append(_attention(q_c, [(k_c, v_c)], score_bound,
                                            tq=tq_ctx).reshape(b * n_ctx, d))
            mixed, attn = [s0], (attn_outs, m_all[1], attn_w_o_b, j)
        else:
            mixed = [_conv_mixer(s0, lat_seg, m_lat[1], norm_g_rows, conv_w_in_b, conv_w,
                                 conv_w_out_b, i, j, tm=tm).reshape(b * n, d)]
            if not last:
                mixed.append(_conv_mixer(s0, ctx_seg, m_ctx[1], norm_g_rows, conv_w_in_b, conv_w,
                                         conv_w_out_b, i, j, tm=n_ctx).reshape(b * n_ctx, d))
            attn = None
        stream = [ffn(mixed, m_all[2], i, 1, not last, attn=attn,
                      final_g=final_g if last else None)]
    return stream[0][:b * n].reshape(b, n, d)
```

```python
import functools
import math

import jax
import jax.numpy as jnp
from jax import lax
from jax.experimental import pallas as pl
from jax.experimental.pallas import tpu as pltpu

D_MODEL = 1024
DEPTH = 4
GRID_W = 64
N_MIXERS = 2
HEAD_DIM = 128
N_HEADS = D_MODEL // HEAD_DIM
N_KV_HEADS = N_HEADS // 4
GQA_GROUP = N_HEADS // N_KV_HEADS
KV_DIM = N_KV_HEADS * HEAD_DIM
QKV_DIM = (N_HEADS + 2 * N_KV_HEADS) * HEAD_DIM
ROPE_THETA = 10000.0
ROPE_FREQS = HEAD_DIM // 4
D_FF = 2816
N_SUBLAYERS = 3
N_MOD = 3
NORM_EPS = 1e-6
MACARON_WEIGHT = 0.5

VMEM_LIMIT_BYTES = 56 * 1024 * 1024
MOD_ROWS = 16
ADA_COL_BLOCK = 1024
FFN_CHUNKS = ((0, 1536), (1536, D_FF))
FFN_SUB_ROWS = 512


def _params(n_grid_dims):
    return pltpu.CompilerParams(
        dimension_semantics=("arbitrary",) * n_grid_dims,
        vmem_limit_bytes=VMEM_LIMIT_BYTES)


def _resident(shape, at=()):
    index = tuple(at) + (0,) * len(shape)
    return pl.BlockSpec((None,) * len(at) + tuple(shape), lambda *_: index,
                        pipeline_mode=pl.Buffered(1))


def _sigmoid(x):
    return 1.0 / (1.0 + jnp.exp(-x))


def _norm_modulate(x, g, mod_ref):
    ms = jnp.mean(x * x, axis=-1, keepdims=True)
    a = g * (1.0 + mod_ref[1:2, :])
    h = (x * lax.rsqrt(ms + NORM_EPS)) * a + mod_ref[0:1, :]
    return h.astype(jnp.bfloat16)


def _dot(a, b):
    return jnp.dot(a, b, preferred_element_type=jnp.float32)


def _ada_kernel(c_ref, w_ref, b_ref, o_ref):
    c = c_ref[...]
    s = (c * _sigmoid(c)).astype(jnp.bfloat16)
    o_ref[...] = _dot(s, w_ref[...].astype(jnp.bfloat16)) + b_ref[...]


def _ada_mods(cvec, ada_w, ada_b):
    depth, d, n = ada_w.shape
    return pl.pallas_call(
        _ada_kernel,
        grid=(depth, n // ADA_COL_BLOCK),
        in_specs=[
            pl.BlockSpec((MOD_ROWS, d), lambda i, j: (0, 0)),
            pl.BlockSpec((None, d, ADA_COL_BLOCK), lambda i, j: (i, 0, j)),
            pl.BlockSpec((None, 1, ADA_COL_BLOCK), lambda i, j: (i, 0, j)),
        ],
        out_specs=pl.BlockSpec((None, MOD_ROWS, ADA_COL_BLOCK), lambda i, j: (i, 0, j)),
        out_shape=jax.ShapeDtypeStruct((depth, MOD_ROWS, n), jnp.float32),
        compiler_params=_params(2),
        name="ada_mods",
    )(cvec, ada_w, ada_b.reshape(depth, 1, n))


def _mod_spec(per_group):
    if per_group:
        return pl.BlockSpec((None, N_MOD, D_MODEL), lambda b, j: (b, 0, 0))
    return pl.BlockSpec((None, N_MOD, D_MODEL), lambda b, j: (0, 0, 0))


def _row_spec(tm, width=D_MODEL):
    return pl.BlockSpec((None, tm, width), lambda b, j: (b, j, 0))


def _ffn_kernel(*refs, n_x, n_ao, n_lat_tiles, final_norm):
    refs = list(refs)
    x_refs, refs = refs[:n_x], refs[n_x:]
    mod_ref, g_ref, w_in_ref, w_out_ref = refs[:4]
    refs = refs[4:]
    if n_ao:
        ao_refs, (mix_mod_ref, wo_ref), refs = refs[:n_ao], refs[n_ao:n_ao + 2], refs[n_ao + 2:]
    if final_norm:
        fg_ref = refs.pop(0)
    (o_ref,) = refs
    is_lat = pl.program_id(0) < n_lat_tiles

    def pick(srcs, rows):
        if len(srcs) == 1:
            return srcs[0][rows, :]
        return jnp.where(is_lat, srcs[0][rows, :], srcs[1][rows, :])

    tm = o_ref.shape[0]
    sub = min(tm, FFN_SUB_ROWS)
    subs = [pl.ds(r, sub) for r in range(0, tm, sub)]
    if n_ao or n_x > 1:
        for rows in subs:
            xs = pick(x_refs, rows)
            if n_ao:
                xs = xs + mix_mod_ref[2:3, :] * _dot(pick(ao_refs, rows), wo_ref[...])
            o_ref[rows, :] = xs
        res_ref = o_ref
    else:
        res_ref = x_refs[0]
    hs = [_norm_modulate(res_ref[rows, :], g_ref[...], mod_ref) for rows in subs]
    for rows, h in zip(subs, hs):
        acc = None
        for lo, hi in FFN_CHUNKS:
            gate = _dot(h, w_in_ref[:, lo:hi])
            up = _dot(h, w_in_ref[:, D_FF + lo:D_FF + hi])
            act = (gate * _sigmoid(gate) * up).astype(jnp.bfloat16)
            part = _dot(act, w_out_ref[lo:hi, :])
            acc = part if acc is None else acc + part
        y = res_ref[rows, :] + (MACARON_WEIGHT * mod_ref[2:3, :]) * acc
        if final_norm:
            ms = jnp.mean(y * y, axis=-1, keepdims=True)
            y = (y * lax.rsqrt(ms + NORM_EPS)) * fg_ref[...]
        o_ref[rows, :] = y


def _tile_specs(n_src, tm, width, n_lat_tiles):
    if n_src == 1:
        return [pl.BlockSpec((tm, width), lambda t: (t, 0))]
    return [pl.BlockSpec((tm, width), lambda t: (jnp.minimum(t, n_lat_tiles - 1), 0)),
            pl.BlockSpec((tm, width), lambda t: (jnp.maximum(t - n_lat_tiles, 0), 0))]


def _ffn(xs, mod, norm_g, w_in, w_out, layer, which, *, tm, n_tiles, n_lat_tiles, tiles_per_group,
         attn=None, final_g=None):
    d = xs[0].shape[1]
    mod_spec = pl.BlockSpec((None, N_MOD, d), lambda t: (t // tiles_per_group, 0, 0))
    in_specs = _tile_specs(len(xs), tm, d, n_lat_tiles) + [
        mod_spec, _resident((1, d), at=(layer * N_SUBLAYERS + 2 * which,)),
        _resident(w_in.shape[2:], at=(layer, which)), _resident(w_out.shape[2:], at=(layer, which))]
    args = list(xs) + [mod, norm_g, w_in, w_out]
    n_ao = 0
    if attn is not None:
        attn_outs, mix_mod, w_o, attn_which = attn
        n_ao = len(attn_outs)
        in_specs += _tile_specs(n_ao, tm, d, n_lat_tiles) + [
            mod_spec, _resident(w_o.shape[1:], at=(attn_which,))]
        args += list(attn_outs) + [mix_mod, w_o]
    if final_g is not None:
        in_specs.append(_resident((1, d)))
        args.append(final_g.reshape(1, d))
    return pl.pallas_call(
        functools.partial(_ffn_kernel, n_x=len(xs), n_ao=n_ao, n_lat_tiles=n_lat_tiles,
                          final_norm=final_g is not None),
        grid=(n_tiles,),
        in_specs=in_specs,
        out_specs=pl.BlockSpec((tm, d), lambda t: (t, 0)),
        out_shape=jax.ShapeDtypeStruct((n_tiles * tm, d), jnp.float32),
        compiler_params=_params(1),
        name="ffn",
    )(*args)


HALO = 8


CONV_CHUNKS = ((0, 512), (512, D_MODEL))
CONV_PASS_ROWS = 1024


def _conv_kernel(x_ref, xp_ref, xn_ref, mod_ref, g_ref, w_in_ref, cw_ref, w_out_ref, o_ref):
    j = pl.program_id(1)
    nj = pl.num_programs(1)
    d = D_MODEL
    tm = x_ref.shape[0]
    sub = min(tm, CONV_PASS_ROWS)
    n_ext = sub + 2 * HALO
    main = slice(HALO, HALO + sub)
    row = lax.broadcasted_iota(jnp.int32, (n_ext, 1), 0)
    for r in range(0, tm, sub):
        first, final = r == 0, r + sub == tm
        x_ext = jnp.concatenate([xp_ref[...] if first else x_ref[r - HALO:r, :],
                                 x_ref[r:r + sub, :],
                                 xn_ref[...] if final else x_ref[r + sub:r + sub + HALO, :]], axis=0)
        h = _norm_modulate(x_ext, g_ref[...], mod_ref)
        valid = None
        if first:
            valid = (row >= HALO) | (j > 0)
        if final:
            ok_next = (row < HALO + sub) | (j < nj - 1)
            valid = ok_next if valid is None else valid & ok_next
        acc = None
        for lo, hi in CONV_CHUNKS:
            b_gate = _dot(h, w_in_ref[:, lo:hi])[main]
            u = _dot(h, w_in_ref[:, d + lo:d + hi]) * _dot(h, w_in_ref[:, 2 * d + lo:2 * d + hi])
            if valid is not None:
                u = jnp.where(valid, u, 0.0)
            y = (cw_ref[0:1, lo:hi] * pltpu.roll(u, 1, 0)[main] + cw_ref[1:2, lo:hi] * u[main]
                 + cw_ref[2:3, lo:hi] * pltpu.roll(u, n_ext - 1, 0)[main])
            part = _dot((b_gate * y).astype(jnp.bfloat16), w_out_ref[lo:hi, :])
            acc = part if acc is None else acc + part
        o_ref[r:r + sub, :] = x_ref[r:r + sub, :] + mod_ref[2:3, :] * acc


def _segment_spec(seg, block_rows, width, index_in_seq):
    row0, _, t = seg
    per_seq = t // block_rows
    base = row0 // block_rows
    return pl.BlockSpec((block_rows, width), lambda b, j: (base + b * per_seq + index_in_seq(j), 0))


def _conv_mixer(s, seg, mod, norm_g, w_in, conv_w, w_out, layer, which, *, tm):
    _, groups, t = seg
    d = s.shape[1]
    nj = t // tm
    hb = tm // HALO
    last_halo_block = t // HALO - 1
    prev_spec = _segment_spec(seg, HALO, d, lambda j: jnp.maximum(j * hb - 1, 0))
    next_spec = _segment_spec(seg, HALO, d, lambda j: jnp.minimum((j + 1) * hb, last_halo_block))
    return pl.pallas_call(
        _conv_kernel,
        grid=(groups, nj),
        in_specs=[_segment_spec(seg, tm, d, lambda j: j), prev_spec, next_spec,
                  _mod_spec(mod.shape[0] > 1),
                  _resident((1, d), at=(layer * N_SUBLAYERS + 1,)),
                  _resident(w_in.shape[1:], at=(which,)), _resident(conv_w.shape[1:], at=(which,)),
                  _resident(w_out.shape[1:], at=(which,))],
        out_specs=_row_spec(tm),
        out_shape=jax.ShapeDtypeStruct((groups, t, d), jnp.float32),
        compiler_params=_params(2),
        name="conv_mixer",
    )(s, s, s, mod, norm_g, w_in, conv_w, w_out)


LANES = 128
def _qkv_kernel(x_ref, mod_ref, g_ref, wt_ref, qg_ref, kg_ref, *rest, rope):
    if rope:
        cos_ref, sin_ref, q_ref, k_ref, v_ref = rest
        cos, sin = cos_ref[...], sin_ref[...]
    else:
        q_ref, k_ref, v_ref = rest
    tm = x_ref.shape[0]
    h = _norm_modulate(x_ref[...], g_ref[...], mod_ref)
    yt = lax.dot_general(wt_ref[...], h, (((1,), (1,)), ((), ())),
                         preferred_element_type=jnp.float32)
    qg = jnp.tile(qg_ref[...], (1, tm // LANES))
    kg = jnp.tile(kg_ref[...], (1, tm // LANES))
    half = HEAD_DIM // 2
    for hd in range(N_HEADS + N_KV_HEADS):
        xh = yt[hd * HEAD_DIM:(hd + 1) * HEAD_DIM, :]
        is_q = hd < N_HEADS
        ms = jnp.mean(xh * xh, axis=0, keepdims=True)
        xh = (xh * lax.rsqrt(ms + NORM_EPS)) * (qg if is_q else kg)
        if rope:
            x1, x2 = xh[:half], xh[half:]
            xh = jnp.concatenate([x1 * cos - x2 * sin, x2 * cos + x1 * sin], axis=0)
        if is_q:
            q_ref[hd * HEAD_DIM:(hd + 1) * HEAD_DIM, :] = xh.astype(q_ref.dtype)
        else:
            kd = hd - N_HEADS
            k_ref[:, kd * HEAD_DIM:(kd + 1) * HEAD_DIM] = xh.T.astype(k_ref.dtype)
    v_ref[...] = yt[D_MODEL + KV_DIM:, :].astype(v_ref.dtype)


def _qkv_proj(s, seg, mod, norm_g, wt_qkv, q_g, k_g, rope_tabs, layer, which, *, tm):
    _, groups, t = seg
    d = s.shape[1]
    x = s
    rope = rope_tabs is not None
    in_specs = [_segment_spec(seg, tm, d, lambda j: j), _mod_spec(mod.shape[0] > 1),
                _resident((1, d), at=(layer * N_SUBLAYERS + 1,)),
                _resident(wt_qkv.shape[1:], at=(which,)),
                _resident((HEAD_DIM, LANES), at=(which,)), _resident((HEAD_DIM, LANES), at=(which,))]
    args = [x, mod, norm_g, wt_qkv, q_g, k_g]
    if rope:
        in_specs += [pl.BlockSpec((HEAD_DIM // 2, tm), lambda b, j: (0, j))] * 2
        args += list(rope_tabs)
    return pl.pallas_call(
        functools.partial(_qkv_kernel, rope=rope),
        grid=(groups, t // tm),
        in_specs=in_specs,
        out_specs=[pl.BlockSpec((None, d, tm), lambda b, j: (b, 0, j)), _row_spec(tm, KV_DIM),
                   pl.BlockSpec((None, KV_DIM, tm), lambda b, j: (b, 0, j))],
        out_shape=[jax.ShapeDtypeStruct((groups, d, t), jnp.bfloat16),
                   jax.ShapeDtypeStruct((groups, t, KV_DIM), jnp.bfloat16),
                   jax.ShapeDtypeStruct((groups, KV_DIM, t), jnp.bfloat16)],
        compiler_params=_params(2),
        name="qkv_proj",
    )(*args)


ATTN_KEY_CHUNK = 2048
ATTN_Q_PASS = 256


def _attn_kernel(use_bound_ref, shift_ref, q_ref, *rest, n_src):
    kv_refs, o_ref = rest[:2 * n_src], rest[2 * n_src]
    chunks = []
    for src in range(n_src):
        k_ref, vt_ref = kv_refs[2 * src], kv_refs[2 * src + 1]
        length = k_ref.shape[0]
        tk = min(ATTN_KEY_CHUNK, length)
        chunks += [(k_ref.at[lo:lo + tk, :], vt_ref.at[:, lo:lo + tk]) for lo in range(0, length, tk)]
    tq = min(q_ref.shape[1], ATTN_Q_PASS)

    def passes(bounded):
        for q0 in range(0, q_ref.shape[1], tq):
            _attn_pass(shift_ref, q_ref.at[:, q0:q0 + tq], chunks, o_ref.at[q0:q0 + tq, :], bounded)

    use_bound = use_bound_ref[0, 0] != 0
    pl.when(use_bound)(functools.partial(passes, True))
    pl.when(jnp.logical_not(use_bound))(functools.partial(passes, False))


def _attn_pass(shift_ref, q_ref, chunks, o_ref, bounded):
    tq = q_ref.shape[1]
    qt = jnp.concatenate([q_ref[g * HEAD_DIM:(g + 1) * HEAD_DIM, :] for g in range(GQA_GROUP)],
                         axis=1)

    def scores(c):
        return _dot(chunks[c][0][...], qt)

    m = l = acc = None
    s_next = scores(0)
    for c, (_, vt_c) in enumerate(chunks):
        s = s_next
        if c + 1 < len(chunks):
            s_next = scores(c + 1)
        if bounded:
            p = jnp.exp2(s - shift_ref[0, 0])
            l_c = jnp.sum(p.reshape(p.shape[0] // 8, 8, p.shape[1]), axis=0)
            acc_c = _dot(vt_c[...], p.astype(jnp.bfloat16))
            l = l_c if l is None else l + l_c
            acc = acc_c if acc is None else acc + acc_c
            continue
        s_max = jnp.max(s, axis=0, keepdims=True)
        if m is None:
            m = s_max
            p = jnp.exp2(s - m)
            l = jnp.sum(p, axis=0, keepdims=True)
            acc = _dot(vt_c[...], p.astype(jnp.bfloat16))
        else:
            m_new = jnp.maximum(m, s_max)
            alpha = jnp.exp2(m - m_new)
            p = jnp.exp2(s - m_new)
            l = alpha * l + jnp.sum(p, axis=0, keepdims=True)
            acc = alpha * acc + _dot(vt_c[...], p.astype(jnp.bfloat16))
            m = m_new
    if bounded:
        l = jnp.sum(l, axis=0, keepdims=True)
    o = (acc / l).T
    for g in range(GQA_GROUP):
        o_ref[:, g * HEAD_DIM:(g + 1) * HEAD_DIM] = o[g * tq:(g + 1) * tq].astype(o_ref.dtype)


MAX_SAFE_SCORE_BOUND = 40.0


def _attention(qt, kv_sources, score_bound, *, tq):
    b, d, n = qt.shape
    gw = GQA_GROUP * HEAD_DIM
    use_bound = (score_bound <= MAX_SAFE_SCORE_BOUND).astype(jnp.int32).reshape(1, 1)
    in_specs = [pl.BlockSpec(memory_space=pltpu.SMEM), pl.BlockSpec(memory_space=pltpu.SMEM),
                pl.BlockSpec((None, gw, tq), lambda bi, kh, j: (bi, kh, j))]
    args = [use_bound, score_bound.reshape(1, 1), qt]
    for k, vt in kv_sources:
        length = k.shape[1]
        in_specs.append(pl.BlockSpec((None, length, HEAD_DIM), lambda bi, kh, j: (bi, 0, kh)))
        in_specs.append(pl.BlockSpec((None, HEAD_DIM, length), lambda bi, kh, j: (bi, kh, 0)))
        args += [k, vt]
    return pl.pallas_call(
        functools.partial(_attn_kernel, n_src=len(kv_sources)),
        grid=(b, N_KV_HEADS, n // tq),
        in_specs=in_specs,
        out_specs=pl.BlockSpec((None, tq, gw), lambda bi, kh, j: (bi, j, kh)),
        out_shape=jax.ShapeDtypeStruct((b, n, d), jnp.bfloat16),
        compiler_params=_params(3),
        name="attention",
    )(*args)


def _head_perm(a, axis):
    shape = a.shape
    a = a.reshape(shape[:axis] + (2, 2, ROPE_FREQS) + shape[axis + 1:])
    return jnp.swapaxes(a, axis, axis + 1).reshape(shape)


def _rope_tables(n):
    pos = jnp.arange(n, dtype=jnp.int32)
    row = (pos // GRID_W).astype(jnp.float32)
    col = (pos % GRID_W).astype(jnp.float32)
    inv_freq = ROPE_THETA ** (-jnp.arange(ROPE_FREQS, dtype=jnp.float32) / ROPE_FREQS)
    ang = jnp.concatenate([inv_freq[:, None] * row[None, :], inv_freq[:, None] * col[None, :]],
                          axis=0)
    return jnp.cos(ang), jnp.sin(ang)


def _tiles(n, n_ctx_total):
    tm = 1024
    assert n % tm == 0 and n_ctx_total % tm == 0, (n, n_ctx_total)
    return tm, tm


def kernel(x, c, ctx, c_ctx, ada_w, ada_b, norm_g, final_g, ffn_w_in, ffn_w_out,
           conv_w_in, conv_w, conv_w_out, attn_w_qkv, attn_q_g, attn_k_g, attn_w_o):
    b, n, d = x.shape
    n_ctx = ctx.shape[1]
    bf = jnp.bfloat16
    ffn_w_in_b, ffn_w_out_b = ffn_w_in.astype(bf), ffn_w_out.astype(bf)
    conv_w_in_b, conv_w_out_b = conv_w_in.astype(bf), conv_w_out.astype(bf)
    attn_w_o_b = attn_w_o.astype(bf)
    n_attn = attn_q_g.shape[0]
    n_qk = D_MODEL + KV_DIM
    w_qk = _head_perm(attn_w_qkv[:, :, :n_qk].reshape(n_attn, d, N_HEADS + N_KV_HEADS, HEAD_DIM), 3)
    w_qkv_perm = jnp.concatenate([w_qk.reshape(n_attn, d, n_qk), attn_w_qkv[:, :, n_qk:]], axis=2)
    attn_wt_qkv_b = w_qkv_perm.transpose(0, 2, 1).astype(bf)
    q_scale = HEAD_DIM ** -0.5 * math.log2(math.e)
    q_g = jnp.broadcast_to((_head_perm(attn_q_g, 1) * q_scale)[:, :, None],
                           (n_attn, HEAD_DIM, LANES))
    k_g = jnp.broadcast_to(_head_perm(attn_k_g, 1)[:, :, None], (n_attn, HEAD_DIM, LANES))
    norm_g_rows = norm_g.reshape(DEPTH * N_SUBLAYERS, 1, d)

    cvec = jnp.zeros((MOD_ROWS, d), jnp.float32).at[:b].set(c).at[b].set(c_ctx)
    mods = _ada_mods(cvec, ada_w, ada_b).reshape(DEPTH, MOD_ROWS, N_SUBLAYERS, N_MOD, d)
    rope_tabs = _rope_tables(n)

    tm, tq = _tiles(n, b * n_ctx)
    tq_ctx = min(tq, n_ctx)
    n_lat_tiles = b * n // tm
    n_all_tiles = n_lat_tiles + b * n_ctx // tm
    lat_seg = (0, b, n)
    ctx_seg = (b * n, b, n_ctx)

    def ffn(xs, mod, layer, which, with_ctx, attn=None, final_g=None):
        return _ffn(xs, mod, norm_g_rows, ffn_w_in_b, ffn_w_out_b, layer, which, tm=tm,
                    n_tiles=n_all_tiles if with_ctx else n_lat_tiles, n_lat_tiles=n_lat_tiles,
                    tiles_per_group=n // tm, attn=attn, final_g=final_g)

    stream = [x.reshape(b * n, d), ctx.reshape(b * n_ctx, d)]
    for i in range(DEPTH):
        is_attn = (i % N_MIXERS) == 1
        last = i == DEPTH - 1
        ctx_feeds_mixer = (not last) or is_attn
        j = i // N_MIXERS
        m_all = [mods[i, :b + 1, s] for s in range(N_SUBLAYERS)]
        m_lat = [m[:b] for m in m_all]
        m_ctx = [m[b:] for m in m_all]
        s0 = ffn(stream if ctx_feeds_mixer else stream[:1], m_all[0], i, 0, ctx_feeds_mixer)
        if is_attn:
            q_l, k_l, v_l = _qkv_proj(s0, lat_seg, m_lat[1], norm_g_rows, attn_wt_qkv_b, q_g, k_g,
                                      rope_tabs, i, j, tm=tm)
            q_c, k_c, v_c = _qkv_proj(s0, ctx_seg, m_ctx[1], norm_g_rows, attn_wt_qkv_b, q_g, k_g,
                                      None, i, j, tm=n_ctx)
            score_bound = (1.02 * HEAD_DIM * q_scale * jnp.max(jnp.abs(attn_q_g[j]))
                           * jnp.max(jnp.abs(attn_k_g[j])))
            attn_outs = [_attention(q_l, [(k_c, v_c), (k_l, v_l)], score_bound,
                                    tq=tq).reshape(b * n, d)]
            if not last:
                attn_outs.append(_attention(q_c, [(k_c, v_c)], score_bound,
                                            tq=tq_ctx).reshape(b * n_ctx, d))
            mixed, attn = [s0], (attn_outs, m_all[1], attn_w_o_b, j)
        else:
            mixed = [_conv_mixer(s0, lat_seg, m_lat[1], norm_g_rows, conv_w_in_b, conv_w,
                                 conv_w_out_b, i, j, tm=tm).reshape(b * n, d)]
            if not last:
                mixed.append(_conv_mixer(s0, ctx_seg, m_ctx[1], norm_g_rows, conv_w_in_b, conv_w,
                                         conv_w_out_b, i, j, tm=n_ctx).reshape(b * n_ctx, d))
            attn = None
        stream = [ffn(mixed, m_all[2], i, 1, not last, attn=attn,
                      final_g=final_g if last else None)]
    return stream[0][:b * n].reshape(b, n, d)
```

```python
import functools
import math

import jax
import jax.numpy as jnp
from jax import lax
from jax.experimental import pallas as pl
from jax.experimental.pallas import tpu as pltpu

D_MODEL = 1024
DEPTH = 4
GRID_W = 64
N_MIXERS = 2
HEAD_DIM = 128
N_HEADS = D_MODEL // HEAD_DIM
N_KV_HEADS = N_HEADS // 4
GQA_GROUP = N_HEADS // N_KV_HEADS
KV_DIM = N_KV_HEADS * HEAD_DIM
QKV_DIM = (N_HEADS + 2 * N_KV_HEADS) * HEAD_DIM
ROPE_THETA = 10000.0
ROPE_FREQS = HEAD_DIM // 4
D_FF = 2816
N_SUBLAYERS = 3
N_MOD = 3
NORM_EPS = 1e-6
MACARON_WEIGHT = 0.5

VMEM_LIMIT_BYTES = 56 * 1024 * 1024
MOD_ROWS = 16
ADA_COL_BLOCK = 1024
FFN_CHUNKS = ((0, 1536), (1536, D_FF))
FFN_SUB_ROWS = 512


def _params(n_grid_dims):
    return pltpu.CompilerParams(
        dimension_semantics=("arbitrary",) * n_grid_dims,
        vmem_limit_bytes=VMEM_LIMIT_BYTES)


def _resident(shape, at=()):
    index = tuple(at) + (0,) * len(shape)
    return pl.BlockSpec((None,) * len(at) + tuple(shape), lambda *_: index,
                        pipeline_mode=pl.Buffered(1))


def _sigmoid(x):
    return 1.0 / (1.0 + jnp.exp(-x))


def _norm_modulate(x, g, mod_ref):
    ms = jnp.mean(x * x, axis=-1, keepdims=True)
    a = g * (1.0 + mod_ref[1:2, :])
    h = (x * lax.rsqrt(ms + NORM_EPS)) * a + mod_ref[0:1, :]
    return h.astype(jnp.bfloat16)


def _dot(a, b):
    return jnp.dot(a, b, preferred_element_type=jnp.float32)


def _ada_kernel(c_ref, w_ref, b_ref, o_ref):
    c = c_ref[...]
    s = (c * _sigmoid(c)).astype(jnp.bfloat16)
    o_ref[...] = _dot(s, w_ref[...].astype(jnp.bfloat16)) + b_ref[...]


def _ada_mods(cvec, ada_w, ada_b):
    depth, d, n = ada_w.shape
    return pl.pallas_call(
        _ada_kernel,
        grid=(depth, n // ADA_COL_BLOCK),
        in_specs=[
            pl.BlockSpec((MOD_ROWS, d), lambda i, j: (0, 0)),
            pl.BlockSpec((None, d, ADA_COL_BLOCK), lambda i, j: (i, 0, j)),
            pl.BlockSpec((None, 1, ADA_COL_BLOCK), lambda i, j: (i, 0, j)),
        ],
        out_specs=pl.BlockSpec((None, MOD_ROWS, ADA_COL_BLOCK), lambda i, j: (i, 0, j)),
        out_shape=jax.ShapeDtypeStruct((depth, MOD_ROWS, n), jnp.float32),
        compiler_params=_params(2),
        name="ada_mods",
    )(cvec, ada_w, ada_b.reshape(depth, 1, n))


def _mod_spec(per_group):
    if per_group:
        return pl.BlockSpec((None, N_MOD, D_MODEL), lambda b, j: (b, 0, 0))
    return pl.BlockSpec((None, N_MOD, D_MODEL), lambda b, j: (0, 0, 0))


def _row_spec(tm, width=D_MODEL):
    return pl.BlockSpec((None, tm, width), lambda b, j: (b, j, 0))


def _ffn_kernel(*refs, n_x, n_ao, n_lat_tiles, final_norm):
    refs = list(refs)
    x_refs, refs = refs[:n_x], refs[n_x:]
    mod_ref, g_ref, w_in_ref, w_out_ref = refs[:4]
    refs = refs[4:]
    if n_ao:
        ao_refs, (mix_mod_ref, wo_ref), refs = refs[:n_ao], refs[n_ao:n_ao + 2], refs[n_ao + 2:]
    if final_norm:
        fg_ref = refs.pop(0)
    (o_ref,) = refs
    is_lat = pl.program_id(0) < n_lat_tiles

    def pick(srcs, rows):
        if len(srcs) == 1:
            return srcs[0][rows, :]
        return jnp.where(is_lat, srcs[0][rows, :], srcs[1][rows, :])

    tm = o_ref.shape[0]
    sub = min(tm, FFN_SUB_ROWS)
    subs = [pl.ds(r, sub) for r in range(0, tm, sub)]
    if n_ao or n_x > 1:
        for rows in subs:
            xs = pick(x_refs, rows)
            if n_ao:
                xs = xs + mix_mod_ref[2:3, :] * _dot(pick(ao_refs, rows), wo_ref[...])
            o_ref[rows, :] = xs
        res_ref = o_ref
    else:
        res_ref = x_refs[0]
    hs = [_norm_modulate(res_ref[rows, :], g_ref[...], mod_ref) for rows in subs]
    for rows, h in zip(subs, hs):
        acc = None
        for lo, hi in FFN_CHUNKS:
            gate = _dot(h, w_in_ref[:, lo:hi])
            up = _dot(h, w_in_ref[:, D_FF + lo:D_FF + hi])
            act = (gate * _sigmoid(gate) * up).astype(jnp.bfloat16)
            part = _dot(act, w_out_ref[lo:hi, :])
            acc = part if acc is None else acc + part
        y = res_ref[rows, :] + (MACARON_WEIGHT * mod_ref[2:3, :]) * acc
        if final_norm:
            ms = jnp.mean(y * y, axis=-1, keepdims=True)
            y = (y * lax.rsqrt(ms + NORM_EPS)) * fg_ref[...]
        o_ref[rows, :] = y


def _tile_specs(n_src, tm, width, n_lat_tiles):
    if n_src == 1:
        return [pl.BlockSpec((tm, width), lambda t: (t, 0))]
    return [pl.BlockSpec((tm, width), lambda t: (jnp.minimum(t, n_lat_tiles - 1), 0)),
            pl.BlockSpec((tm, width), lambda t: (jnp.maximum(t - n_lat_tiles, 0), 0))]


def _ffn(xs, mod, norm_g, w_in, w_out, layer, which, *, tm, n_tiles, n_lat_tiles, tiles_per_group,
         attn=None, final_g=None):
    d = xs[0].shape[1]
    mod_spec = pl.BlockSpec((None, N_MOD, d), lambda t: (t // tiles_per_group, 0, 0))
    in_specs = _tile_specs(len(xs), tm, d, n_lat_tiles) + [
        mod_spec, _resident((1, d), at=(layer * N_SUBLAYERS + 2 * which,)),
        _resident(w_in.shape[2:], at=(layer, which)), _resident(w_out.shape[2:], at=(layer, which))]
    args = list(xs) + [mod, norm_g, w_in, w_out]
    n_ao = 0
    if attn is not None:
        attn_outs, mix_mod, w_o, attn_which = attn
        n_ao = len(attn_outs)
        in_specs += _tile_specs(n_ao, tm, d, n_lat_tiles) + [
            mod_spec, _resident(w_o.shape[1:], at=(attn_which,))]
        args += list(attn_outs) + [mix_mod, w_o]
    if final_g is not None:
        in_specs.append(_resident((1, d)))
        args.append(final_g.reshape(1, d))
    return pl.pallas_call(
        functools.partial(_ffn_kernel, n_x=len(xs), n_ao=n_ao, n_lat_tiles=n_lat_tiles,
                          final_norm=final_g is not None),
        grid=(n_tiles,),
        in_specs=in_specs,
        out_specs=pl.BlockSpec((tm, d), lambda t: (t, 0)),
        out_shape=jax.ShapeDtypeStruct((n_tiles * tm, d), jnp.float32),
        compiler_params=_params(1),
        name="ffn",
    )(*args)


HALO = 8


CONV_CHUNKS = ((0, 512), (512, D_MODEL))


def _conv_kernel(x_ref, xp_ref, xn_ref, mod_ref, g_ref, w_in_ref, cw_ref, w_out_ref, o_ref):
    j = pl.program_id(1)
    nj = pl.num_programs(1)
    d = D_MODEL
    tm = x_ref.shape[0]
    n_ext = tm + 2 * HALO
    main = slice(HALO, HALO + tm)
    x_ext = jnp.concatenate([xp_ref[...], x_ref[...], xn_ref[...]], axis=0)
    h = _norm_modulate(x_ext, g_ref[...], mod_ref)
    row = lax.broadcasted_iota(jnp.int32, (n_ext, 1), 0)
    valid = ((row >= HALO) | (j > 0)) & ((row < HALO + tm) | (j < nj - 1))
    acc = None
    for lo, hi in CONV_CHUNKS:
        w_b, w_c, w_v = (w_in_ref[:, k * d + lo:k * d + hi].astype(jnp.bfloat16) for k in range(3))
        b_gate = _dot(h, w_b)[main]
        u = _dot(h, w_c) * _dot(h, w_v)
        u = jnp.where(valid, u, 0.0)
        y = (cw_ref[0:1, lo:hi] * pltpu.roll(u, 1, 0)[main] + cw_ref[1:2, lo:hi] * u[main]
             + cw_ref[2:3, lo:hi] * pltpu.roll(u, n_ext - 1, 0)[main])
        part = _dot((b_gate * y).astype(jnp.bfloat16), w_out_ref[lo:hi, :].astype(jnp.bfloat16))
        acc = part if acc is None else acc + part
    o_ref[...] = x_ref[...] + mod_ref[2:3, :] * acc


def _segment_spec(seg, block_rows, width, index_in_seq):
    row0, _, t = seg
    per_seq = t // block_rows
    base = row0 // block_rows
    return pl.BlockSpec((block_rows, width), lambda b, j: (base + b * per_seq + index_in_seq(j), 0))


def _conv_mixer(s, seg, mod, norm_g, w_in, conv_w, w_out, layer, which, *, tm):
    _, groups, t = seg
    d = s.shape[1]
    nj = t // tm
    hb = tm // HALO
    last_halo_block = t // HALO - 1
    prev_spec = _segment_spec(seg, HALO, d, lambda j: jnp.maximum(j * hb - 1, 0))
    next_spec = _segment_spec(seg, HALO, d, lambda j: jnp.minimum((j + 1) * hb, last_halo_block))
    return pl.pallas_call(
        _conv_kernel,
        grid=(groups, nj),
        in_specs=[_segment_spec(seg, tm, d, lambda j: j), prev_spec, next_spec,
                  _mod_spec(mod.shape[0] > 1),
                  _resident((1, d), at=(layer * N_SUBLAYERS + 1,)),
                  _resident(w_in.shape[1:], at=(which,)), _resident(conv_w.shape[1:], at=(which,)),
                  _resident(w_out.shape[1:], at=(which,))],
        out_specs=_row_spec(tm),
        out_shape=jax.ShapeDtypeStruct((groups, t, d), jnp.float32),
        compiler_params=_params(2),
        name="conv_mixer",
    )(s, s, s, mod, norm_g, w_in, conv_w, w_out)


LANES = 128
def _qkv_kernel(x_ref, mod_ref, g_ref, wt_ref, qg_ref, kg_ref, *rest, rope):
    if rope:
        cos_ref, sin_ref, q_ref, k_ref, v_ref = rest
        cos, sin = cos_ref[...], sin_ref[...]
    else:
        q_ref, k_ref, v_ref = rest
    tm = x_ref.shape[0]
    h = _norm_modulate(x_ref[...], g_ref[...], mod_ref)
    yt = lax.dot_general(wt_ref[...], h, (((1,), (1,)), ((), ())),
                         preferred_element_type=jnp.float32)
    qg = jnp.tile(qg_ref[...], (1, tm // LANES))
    kg = jnp.tile(kg_ref[...], (1, tm // LANES))
    half = HEAD_DIM // 2
    for hd in range(N_HEADS + N_KV_HEADS):
        xh = yt[hd * HEAD_DIM:(hd + 1) * HEAD_DIM, :]
        is_q = hd < N_HEADS
        ms = jnp.mean(xh * xh, axis=0, keepdims=True)
        xh = (xh * lax.rsqrt(ms + NORM_EPS)) * (qg if is_q else kg)
        if rope:
            x1, x2 = xh[:half], xh[half:]
            xh = jnp.concatenate([x1 * cos - x2 * sin, x2 * cos + x1 * sin], axis=0)
        if is_q:
            q_ref[hd * HEAD_DIM:(hd + 1) * HEAD_DIM, :] = xh.astype(q_ref.dtype)
        else:
            kd = hd - N_HEADS
            k_ref[:, kd * HEAD_DIM:(kd + 1) * HEAD_DIM] = xh.T.astype(k_ref.dtype)
    v_ref[...] = yt[D_MODEL + KV_DIM:, :].astype(v_ref.dtype)


def _qkv_proj(s, seg, mod, norm_g, wt_qkv, q_g, k_g, rope_tabs, layer, which, *, tm):
    _, groups, t = seg
    d = s.shape[1]
    x = s
    rope = rope_tabs is not None
    in_specs = [_segment_spec(seg, tm, d, lambda j: j), _mod_spec(mod.shape[0] > 1),
                _resident((1, d), at=(layer * N_SUBLAYERS + 1,)),
                _resident(wt_qkv.shape[1:], at=(which,)),
                _resident((HEAD_DIM, LANES), at=(which,)), _resident((HEAD_DIM, LANES), at=(which,))]
    args = [x, mod, norm_g, wt_qkv, q_g, k_g]
    if rope:
        in_specs += [pl.BlockSpec((HEAD_DIM // 2, tm), lambda b, j: (0, j))] * 2
        args += list(rope_tabs)
    return pl.pallas_call(
        functools.partial(_qkv_kernel, rope=rope),
        grid=(groups, t // tm),
        in_specs=in_specs,
        out_specs=[pl.BlockSpec((None, d, tm), lambda b, j: (b, 0, j)), _row_spec(tm, KV_DIM),
                   pl.BlockSpec((None, KV_DIM, tm), lambda b, j: (b, 0, j))],
        out_shape=[jax.ShapeDtypeStruct((groups, d, t), jnp.bfloat16),
                   jax.ShapeDtypeStruct((groups, t, KV_DIM), jnp.bfloat16),
                   jax.ShapeDtypeStruct((groups, KV_DIM, t), jnp.bfloat16)],
        compiler_params=_params(2),
        name="qkv_proj",
    )(*args)


ATTN_KEY_CHUNK = 2048
ATTN_Q_PASS = 256


def _attn_kernel(shift_ref, q_ref, *rest, n_src, bounded):
    kv_refs, o_ref = rest[:2 * n_src], rest[2 * n_src]
    chunks = []
    for src in range(n_src):
        k_ref, vt_ref = kv_refs[2 * src], kv_refs[2 * src + 1]
        length = k_ref.shape[0]
        tk = min(ATTN_KEY_CHUNK, length)
        chunks += [(k_ref.at[lo:lo + tk, :], vt_ref.at[:, lo:lo + tk]) for lo in range(0, length, tk)]
    tq = min(q_ref.shape[1], ATTN_Q_PASS)
    for q0 in range(0, q_ref.shape[1], tq):
        _attn_pass(shift_ref, q_ref.at[:, q0:q0 + tq], chunks, o_ref.at[q0:q0 + tq, :], bounded)


def _attn_pass(shift_ref, q_ref, chunks, o_ref, bounded):
    tq = q_ref.shape[1]
    qt = jnp.concatenate([q_ref[g * HEAD_DIM:(g + 1) * HEAD_DIM, :] for g in range(GQA_GROUP)],
                         axis=1)

    def scores(c):
        return _dot(chunks[c][0][...], qt)

    m = l = acc = None
    s_next = scores(0)
    for c, (_, vt_c) in enumerate(chunks):
        s = s_next
        if c + 1 < len(chunks):
            s_next = scores(c + 1)
        if bounded:
            p = jnp.exp2(s - shift_ref[0, 0])
            l_c = jnp.sum(p.reshape(p.shape[0] // 8, 8, p.shape[1]), axis=0)
            acc_c = _dot(vt_c[...], p.astype(jnp.bfloat16))
            l = l_c if l is None else l + l_c
            acc = acc_c if acc is None else acc + acc_c
            continue
        s_max = jnp.max(s, axis=0, keepdims=True)
        if m is None:
            m = s_max
            p = jnp.exp2(s - m)
            l = jnp.sum(p, axis=0, keepdims=True)
            acc = _dot(vt_c[...], p.astype(jnp.bfloat16))
        else:
            m_new = jnp.maximum(m, s_max)
            alpha = jnp.exp2(m - m_new)
            p = jnp.exp2(s - m_new)
            l = alpha * l + jnp.sum(p, axis=0, keepdims=True)
            acc = alpha * acc + _dot(vt_c[...], p.astype(jnp.bfloat16))
            m = m_new
    if bounded:
        l = jnp.sum(l, axis=0, keepdims=True)
    o = (acc / l).T
    for g in range(GQA_GROUP):
        o_ref[:, g * HEAD_DIM:(g + 1) * HEAD_DIM] = o[g * tq:(g + 1) * tq].astype(o_ref.dtype)


def _attention_call(shift, qt, kv_sources, *, tq, bounded):
    b, d, n = qt.shape
    gw = GQA_GROUP * HEAD_DIM
    in_specs = [pl.BlockSpec(memory_space=pltpu.SMEM),
                pl.BlockSpec((None, gw, tq), lambda bi, kh, j: (bi, kh, j))]
    args = [shift, qt]
    for k, vt in kv_sources:
        length = k.shape[1]
        in_specs.append(pl.BlockSpec((None, length, HEAD_DIM), lambda bi, kh, j: (bi, 0, kh)))
        in_specs.append(pl.BlockSpec((None, HEAD_DIM, length), lambda bi, kh, j: (bi, kh, 0)))
        args += [k, vt]
    return pl.pallas_call(
        functools.partial(_attn_kernel, n_src=len(kv_sources), bounded=bounded),
        grid=(b, N_KV_HEADS, n // tq),
        in_specs=in_specs,
        out_specs=pl.BlockSpec((None, tq, gw), lambda bi, kh, j: (bi, j, kh)),
        out_shape=jax.ShapeDtypeStruct((b, n, d), jnp.bfloat16),
        compiler_params=_params(3),
        name="attention_bounded" if bounded else "attention_online",
    )(*args)


MAX_SAFE_SCORE_BOUND = 40.0


def _attention(q, kv_sources, score_bound, *, tq):
    return lax.cond(
        score_bound <= MAX_SAFE_SCORE_BOUND,
        functools.partial(_attention_call, tq=tq, bounded=True),
        functools.partial(_attention_call, tq=tq, bounded=False),
        score_bound.reshape(1, 1), q, kv_sources)


def _head_perm(a, axis):
    shape = a.shape
    a = a.reshape(shape[:axis] + (2, 2, ROPE_FREQS) + shape[axis + 1:])
    return jnp.swapaxes(a, axis, axis + 1).reshape(shape)


def _rope_tables(n):
    pos = jnp.arange(n, dtype=jnp.int32)
    row = (pos // GRID_W).astype(jnp.float32)
    col = (pos % GRID_W).astype(jnp.float32)
    inv_freq = ROPE_THETA ** (-jnp.arange(ROPE_FREQS, dtype=jnp.float32) / ROPE_FREQS)
    ang = jnp.concatenate([inv_freq[:, None] * row[None, :], inv_freq[:, None] * col[None, :]],
                          axis=0)
    return jnp.cos(ang), jnp.sin(ang)


def _tiles(n, n_ctx_total):
    tm = 1024
    assert n % tm == 0 and n_ctx_total % tm == 0, (n, n_ctx_total)
    return tm, tm


def kernel(x, c, ctx, c_ctx, ada_w, ada_b, norm_g, final_g, ffn_w_in, ffn_w_out,
           conv_w_in, conv_w, conv_w_out, attn_w_qkv, attn_q_g, attn_k_g, attn_w_o):
    b, n, d = x.shape
    n_ctx = ctx.shape[1]
    bf = jnp.bfloat16
    ffn_w_in_b, ffn_w_out_b = ffn_w_in.astype(bf), ffn_w_out.astype(bf)
    conv_w_in_b, conv_w_out_b = conv_w_in, conv_w_out
    attn_w_o_b = attn_w_o.astype(bf)
    n_attn = attn_q_g.shape[0]
    n_qk = D_MODEL + KV_DIM
    w_qk = _head_perm(attn_w_qkv[:, :, :n_qk].reshape(n_attn, d, N_HEADS + N_KV_HEADS, HEAD_DIM), 3)
    w_qkv_perm = jnp.concatenate([w_qk.reshape(n_attn, d, n_qk), attn_w_qkv[:, :, n_qk:]], axis=2)
    attn_wt_qkv_b = w_qkv_perm.transpose(0, 2, 1).astype(bf)
    q_scale = HEAD_DIM ** -0.5 * math.log2(math.e)
    q_g = jnp.broadcast_to((_head_perm(attn_q_g, 1) * q_scale)[:, :, None],
                           (n_attn, HEAD_DIM, LANES))
    k_g = jnp.broadcast_to(_head_perm(attn_k_g, 1)[:, :, None], (n_attn, HEAD_DIM, LANES))
    norm_g_rows = norm_g.reshape(DEPTH * N_SUBLAYERS, 1, d)

    cvec = jnp.zeros((MOD_ROWS, d), jnp.float32).at[:b].set(c).at[b].set(c_ctx)
    mods = _ada_mods(cvec, ada_w, ada_b).reshape(DEPTH, MOD_ROWS, N_SUBLAYERS, N_MOD, d)
    rope_tabs = _rope_tables(n)

    tm, tq = _tiles(n, b * n_ctx)
    tq_ctx = min(tq, n_ctx)
    n_lat_tiles = b * n // tm
    n_all_tiles = n_lat_tiles + b * n_ctx // tm
    lat_seg = (0, b, n)
    ctx_seg = (b * n, b, n_ctx)

    def ffn(xs, mod, layer, which, with_ctx, attn=None, final_g=None):
        return _ffn(xs, mod, norm_g_rows, ffn_w_in_b, ffn_w_out_b, layer, which, tm=tm,
                    n_tiles=n_all_tiles if with_ctx else n_lat_tiles, n_lat_tiles=n_lat_tiles,
                    tiles_per_group=n // tm, attn=attn, final_g=final_g)

    stream = [x.reshape(b * n, d), ctx.reshape(b * n_ctx, d)]
    for i in range(DEPTH):
        is_attn = (i % N_MIXERS) == 1
        last = i == DEPTH - 1
        ctx_feeds_mixer = (not last) or is_attn
        j = i // N_MIXERS
        m_all = [mods[i, :b + 1, s] for s in range(N_SUBLAYERS)]
        m_lat = [m[:b] for m in m_all]
        m_ctx = [m[b:] for m in m_all]
        s0 = ffn(stream if ctx_feeds_mixer else stream[:1], m_all[0], i, 0, ctx_feeds_mixer)
        if is_attn:
            q_l, k_l, v_l = _qkv_proj(s0, lat_seg, m_lat[1], norm_g_rows, attn_wt_qkv_b, q_g, k_g,
                                      rope_tabs, i, j, tm=tm)
            q_c, k_c, v_c = _qkv_proj(s0, ctx_seg, m_ctx[1], norm_g_rows, attn_wt_qkv_b, q_g, k_g,
                                      None, i, j, tm=n_ctx)
            score_bound = (1.02 * HEAD_DIM * q_scale * jnp.max(jnp.abs(attn_q_g[j]))
                           * jnp.max(jnp.abs(attn_k_g[j])))
            attn_outs = [_attention(q_l, [(k_c, v_c), (k_l, v_l)], score_bound,
                                    tq=tq).reshape(b * n, d)]
            if not last:
                attn_outs.append(_attention(q_c, [(k_c, v_c)], score_bound,
                                            tq=tq_ctx).reshape(b * n_ctx, d))
            mixed, attn = [s0], (attn_outs, m_all[1], attn_w_o_b, j)
        else:
            mixed = [_conv_mixer(s0, lat_seg, m_lat[1], norm_g_rows, conv_w_in_b, conv_w,
                                 conv_w_out_b, i, j, tm=tm).reshape(b * n, d)]
            if not last:
                mixed.append(_conv_mixer(s0, ctx_seg, m_ctx[1], norm_g_rows, conv_w_in_b, conv_w,
                                         conv_w_out_b, i, j, tm=n_ctx).reshape(b * n_ctx, d))
            attn = None
        stream = [ffn(mixed, m_all[2], i, 1, not last, attn=attn,
                      final_g=final_g if last else None)]
    return stream[0][:b * n].reshape(b, n, d)
```
